```python
import math
import jax, jax.numpy as jnp
from jax import lax
import numpy as np

D_MODEL = 2048
BATCH = 8
SEQ = 2048
DEPTH = 2

N_MIXERS = 2
N_A = (DEPTH + 1) // 2
N_B = DEPTH // 2
BLK = 128
WIN = 128
KEY_SPAN = BLK + 2 * WIN
A_DH = 64
A_HQ = D_MODEL // A_DH
A_KV = 4
A_G = A_HQ // A_KV
A_QKV = A_HQ * A_DH + 2 * A_KV * A_DH
B_DH = 128
B_H = D_MODEL // (2 * B_DH)
B_QKV = 3 * B_H * 2 * B_DH
D_FF = int(math.ceil(8 * D_MODEL / 3 / 256) * 256)
PLE_DIM = 256
EPS = 1e-6
NEG = -1e30

kernel_name = "hybrid_swa_sink_diffattn_encoder"


def rms_norm(x, g):
    xf = x.astype(jnp.float32)
    y = xf * lax.rsqrt(jnp.mean(xf * xf, axis=-1, keepdims=True) + EPS)
    return (y * g.astype(jnp.float32)).astype(x.dtype)


def alibi_slopes(n):
    return jnp.exp2(-8.0 * jnp.arange(1, n + 1, dtype=jnp.float32) / n)


def windowed_gqa_sink(h, w_qkv, q_gain, k_gain, sink, w_o):
    B, S, _ = h.shape
    nb = S // BLK
    qkv = h @ w_qkv
    nq = A_HQ * A_DH
    nk = A_KV * A_DH
    q = rms_norm(qkv[..., :nq].reshape(B, S, A_KV, A_G, A_DH), q_gain)
    k = rms_norm(qkv[..., nq:nq + nk].reshape(B, S, A_KV, A_DH), k_gain)
    v = qkv[..., nq + nk:].reshape(B, S, A_KV, A_DH)
    qb = q.reshape(B, nb, BLK, A_KV, A_G, A_DH).transpose(1, 0, 3, 4, 2, 5)
    pad = ((0, 0), (0, 0), (WIN, WIN), (0, 0))
    kp = jnp.pad(k.transpose(0, 2, 1, 3), pad)
    vp = jnp.pad(v.transpose(0, 2, 1, 3), pad)
    slopes = alibi_slopes(A_HQ).reshape(A_KV, A_G)
    sink_f = sink.astype(jnp.float32).reshape(A_KV, A_G)
    scale = A_DH ** -0.5

    def block(args):
        qblk, n = args
        start = n * BLK
        kb = lax.dynamic_slice_in_dim(kp, start, KEY_SPAN, axis=2)
        vb = lax.dynamic_slice_in_dim(vp, start, KEY_SPAN, axis=2)
        sc = jnp.einsum('bkgqd,bksd->bkgqs', qblk, kb).astype(jnp.float32) * scale
        t = start + jnp.arange(BLK)
        s = start - WIN + jnp.arange(KEY_SPAN)
        dist = jnp.abs(t[:, None] - s[None, :])
        valid = (dist <= WIN) & (s[None, :] >= 0) & (s[None, :] < S)
        sc = sc - slopes[:, :, None, None] * dist.astype(jnp.float32)
        sc = jnp.where(valid, sc, NEG)
        sink_col = jnp.broadcast_to(sink_f[None, :, :, None, None], sc.shape[:-1] + (1,))
        probs = jax.nn.softmax(jnp.concatenate([sc, sink_col], axis=-1), axis=-1)[..., :-1]
        return jnp.einsum('bkgqs,bksd->bkgqd', probs.astype(vb.dtype), vb)

    o = lax.map(block, (qb, jnp.arange(nb)))
    o = o.transpose(1, 0, 4, 2, 3, 5).reshape(B, S, A_HQ * A_DH)
    return o @ w_o


def diff_attention(h, w_qkv, q_gain, k_gain, lam_vecs, subln, w_o, lambda_init):
    B, S, _ = h.shape
    nb = S // BLK
    qkv = h @ w_qkv
    w = B_H * 2 * B_DH
    q = rms_norm(qkv[..., :w].reshape(B, S, B_H, 2, B_DH), q_gain)
    k = rms_norm(qkv[..., w:2 * w].reshape(B, S, B_H, 2, B_DH), k_gain)
    v = qkv[..., 2 * w:].reshape(B, S, B_H, 2 * B_DH)
    qb = q.reshape(B, nb, BLK, B_H, 2, B_DH).transpose(1, 0, 3, 4, 2, 5)
    kt = k.transpose(0, 2, 3, 1, 4)
    vt = v.transpose(0, 2, 1, 3)
    lv = lam_vecs.astype(jnp.float32)
    lam = jnp.exp(jnp.sum(lv[0] * lv[1])) - jnp.exp(jnp.sum(lv[2] * lv[3])) + lambda_init
    slopes = alibi_slopes(B_H)
    s_pos = jnp.arange(S)
    scale = B_DH ** -0.5

    def block(args):
        qblk, n = args
        sc = jnp.einsum('bhcqd,bhcsd->bhcqs', qblk, kt).astype(jnp.float32) * scale
        t = n * BLK + jnp.arange(BLK)
        dist = jnp.abs(t[:, None] - s_pos[None, :]).astype(jnp.float32)
        sc = sc - (slopes[:, None, None] * dist)[None, :, None]
        probs = jax.nn.softmax(sc, axis=-1)
        wts = probs[:, :, 0] - lam * probs[:, :, 1]
        return jnp.einsum('bhqs,bhse->bhqe', wts.astype(vt.dtype), vt)

    o = lax.map(block, (qb, jnp.arange(nb)))
    o = rms_norm(o, subln) * (1.0 - lambda_init)
    o = o.transpose(1, 0, 3, 2, 4).reshape(B, S, B_H * 2 * B_DH)
    return o @ w_o


def swiglu(h, w_in, w_out):
    gu = h @ w_in
    return (jax.nn.silu(gu[..., :D_FF]) * gu[..., D_FF:]) @ w_out


def setup_inputs(seed: int = 0) -> dict:
    key = jax.random.key(seed)
    ks = jax.random.split(key, 24)
    f32 = jnp.float32

    def nrm(k, shape, scale):
        return jax.random.normal(k, shape, f32) * scale

    def gain(k, shape):
        return 1.0 + 0.02 * jax.random.normal(k, shape, f32)

    return {
        "x": nrm(ks[0], (BATCH, SEQ, D_MODEL), 1.0),
        "p": nrm(ks[1], (DEPTH, BATCH, SEQ, PLE_DIM), 1.0),
        "attn_norm": gain(ks[2], (DEPTH, D_MODEL)),
        "ffn_norm": gain(ks[3], (DEPTH, D_MODEL)),
        "a_w_qkv": nrm(ks[4], (N_A, D_MODEL, A_QKV), D_MODEL ** -0.5),
        "a_q_norm": gain(ks[5], (N_A, A_DH)),
        "a_k_norm": gain(ks[6], (N_A, A_DH)),
        "a_sink": nrm(ks[7], (N_A, A_HQ), 0.5),
        "a_w_o": nrm(ks[8], (N_A, A_HQ * A_DH, D_MODEL), (A_HQ * A_DH) ** -0.5),
        "b_w_qkv": nrm(ks[9], (N_B, D_MODEL, B_QKV), D_MODEL ** -0.5),
        "b_q_norm": gain(ks[10], (N_B, B_DH)),
        "b_k_norm": gain(ks[11], (N_B, B_DH)),
        "b_lambda": nrm(ks[12], (N_B, 4, B_DH), 0.1),
        "b_subln": gain(ks[13], (N_B, 2 * B_DH)),
        "b_w_o": nrm(ks[14], (N_B, B_H * 2 * B_DH, D_MODEL), (B_H * 2 * B_DH) ** -0.5),
        "w_ffn_in": nrm(ks[15], (DEPTH, D_MODEL, 2 * D_FF), D_MODEL ** -0.5),
        "w_ffn_out": nrm(ks[16], (DEPTH, D_FF, D_MODEL), D_FF ** -0.5),
        "ple_w_proj": nrm(ks[17], (DEPTH, PLE_DIM, D_MODEL), PLE_DIM ** -0.5),
        "ple_post_norm": gain(ks[18], (DEPTH, D_MODEL)),
        "ple_gate_norm": gain(ks[19], (DEPTH, D_MODEL)),
        "ple_w_gate": nrm(ks[20], (DEPTH, D_MODEL, D_MODEL), D_MODEL ** -0.5),
    }


def reference(x, p, attn_norm, ffn_norm, a_w_qkv, a_q_norm, a_k_norm, a_sink, a_w_o,
              b_w_qkv, b_q_norm, b_k_norm, b_lambda, b_subln, b_w_o,
              w_ffn_in, w_ffn_out, ple_w_proj, ple_post_norm, ple_gate_norm, ple_w_gate):
    h = x
    for i in range(DEPTH):
        hn = rms_norm(h, attn_norm[i])
        j = i // N_MIXERS
        if i % N_MIXERS == 0:
            mix = windowed_gqa_sink(hn, a_w_qkv[j], a_q_norm[j], a_k_norm[j], a_sink[j], a_w_o[j])
        else:
            lambda_init = 0.8 - 0.6 * math.exp(-0.3 * i)
            mix = diff_attention(hn, b_w_qkv[j], b_q_norm[j], b_k_norm[j], b_lambda[j],
                                 b_subln[j], b_w_o[j], lambda_init)
        h = h + mix
        h = h + swiglu(rms_norm(h, ffn_norm[i]), w_ffn_in[i], w_ffn_out[i])
        gate = jax.nn.sigmoid(rms_norm(h, ple_gate_norm[i]) @ ple_w_gate[i])
        h = h + rms_norm(p[i] @ ple_w_proj[i], ple_post_norm[i]) * gate
    return h
```

```python
import functools
import math

import jax
import jax.numpy as jnp
from jax import lax
from jax.experimental import pallas as pl
from jax.experimental.pallas import tpu as pltpu

D_MODEL = 2048
DEPTH = 2
BLK = 128
WIN = 128
A_DH = 64
A_HQ = D_MODEL // A_DH
A_KV = 4
A_G = A_HQ // A_KV
A_NQ = A_HQ * A_DH
A_NK = A_KV * A_DH
A_QKV = A_NQ + 2 * A_NK
B_DH = 128
B_H = D_MODEL // (2 * B_DH)
B_W = B_H * 2 * B_DH
B_QKV = 3 * B_W
D_FF = int(math.ceil(8 * D_MODEL / 3 / 256) * 256)
PLE_DIM = 256
EPS = 1e-6
NEG = -1e30

V7X_VMEM_LIMIT = 60 * 1024 * 1024
QKV_TM, QKV_TN = 1024, 512
PROJ_TM, PROJ_TN = 1024, 1024
FFN_TM, FFN_TF = 512, 512
PLE_TM = 512
ATTN_B_TQ = 256

F32 = jnp.float32
BF16 = jnp.bfloat16


def _params(*sem):
    return pltpu.CompilerParams(dimension_semantics=sem, vmem_limit_bytes=V7X_VMEM_LIMIT)


def _row_rms(x, gain):
    return x * lax.rsqrt(jnp.mean(x * x, axis=-1, keepdims=True) + EPS) * gain


def _dot(a, b):
    return jnp.dot(a, b, preferred_element_type=F32)


def _dot_nt(a, b):
    return lax.dot_general(a, b, (((1,), (1,)), ((), ())), preferred_element_type=F32)


def _qkv_kernel(x_ref, g_ref, w_ref, cg_ref, cm_ref, bd_ref, o_ref, hn_ref, *, group, n_norm):
    j = pl.program_id(1)

    @pl.when(j == 0)
    def _():
        hn_ref[...] = _row_rms(x_ref[...], g_ref[...]).astype(BF16)

    o = _dot(hn_ref[...], w_ref[...])

    @pl.when(j < n_norm)
    def _():
        ss = _dot((o * o).astype(BF16), bd_ref[...])
        nrm = o * lax.rsqrt(ss * (1.0 / group) + EPS) * cg_ref[...]
        o_ref[...] = jnp.where(cm_ref[...] > 0.0, nrm, o).astype(o_ref.dtype)

    @pl.when(j >= n_norm)
    def _():
        o_ref[...] = o.astype(o_ref.dtype)


def _qkv_proj(x, gain, w, col_gain, col_mask, group, n_norm):
    m, d = x.shape
    n = w.shape[1]
    tm, tn = QKV_TM, QKV_TN
    r = jnp.arange(tn) // group
    bd = (r[:, None] == r[None, :]).astype(BF16)
    return pl.pallas_call(
        functools.partial(_qkv_kernel, group=group, n_norm=n_norm),
        grid=(m // tm, n // tn),
        in_specs=[
            pl.BlockSpec((tm, d), lambda i, j: (i, 0)),
            pl.BlockSpec((1, d), lambda i, j: (0, 0)),
            pl.BlockSpec((d, tn), lambda i, j: (0, j)),
            pl.BlockSpec((1, tn), lambda i, j: (0, j)),
            pl.BlockSpec((1, tn), lambda i, j: (0, j)),
            pl.BlockSpec((tn, tn), lambda i, j: (0, 0)),
        ],
        out_specs=pl.BlockSpec((tm, tn), lambda i, j: (i, j)),
        out_shape=jax.ShapeDtypeStruct((m, n), BF16),
        scratch_shapes=[pltpu.VMEM((tm, d), BF16)],
        compiler_params=_params("parallel", "arbitrary"),
        name="qkv_proj",
    )(x, gain.reshape(1, d), w, col_gain.reshape(1, n), col_mask.reshape(1, n), bd)


def _attn_a_kernel(slope_ref, sink_ref, q_ref, kvp_ref, kvc_ref, kvn_ref, o_ref, *, seq):
    n = pl.program_id(1)
    span = BLK + 2 * WIN
    q = q_ref[...]
    kv = jnp.concatenate([kvp_ref[...], kvc_ref[...], kvn_ref[...]], axis=0)
    ti = lax.broadcasted_iota(jnp.int32, (BLK, span), 0)
    sj = lax.broadcasted_iota(jnp.int32, (BLK, span), 1)
    dist_i = jnp.abs(ti + WIN - sj)
    pos = sj + (n * BLK - WIN)
    valid = (dist_i <= WIN) & (pos >= 0) & (pos < seq)
    dist = dist_i.astype(F32)
    for kvh in range(A_KV):
        k = kv[:, kvh * A_DH:(kvh + 1) * A_DH]
        v = kv[:, A_NK + kvh * A_DH:A_NK + (kvh + 1) * A_DH]
        qs = jnp.concatenate(
            [q[:, (kvh * A_G + g) * A_DH:(kvh * A_G + g + 1) * A_DH] for g in range(A_G)], axis=0)
        s = _dot_nt(qs, k)
        ps, rls = [], []
        for g in range(A_G):
            h = kvh * A_G + g
            sh = jnp.where(valid, s[g * BLK:(g + 1) * BLK] - slope_ref[h] * dist, NEG)
            sink = sink_ref[h]
            mx = jnp.maximum(jnp.max(sh, axis=-1, keepdims=True), sink)
            p = jnp.exp(sh - mx)
            l = jnp.sum(p, axis=-1, keepdims=True) + jnp.exp(sink - mx)
            ps.append(p.astype(BF16))
            rls.append(1.0 / l)
        pv = _dot(jnp.concatenate(ps, axis=0), v)
        for g in range(A_G):
            h = kvh * A_G + g
            o_ref[:, h * A_DH:(h + 1) * A_DH] = (pv[g * BLK:(g + 1) * BLK] * rls[g]).astype(o_ref.dtype)


def _attn_a(qkv, slopes, sink, batch, seq):
    nb = seq // BLK
    kv_col = A_NQ // (2 * A_NK)
    return pl.pallas_call(
        functools.partial(_attn_a_kernel, seq=seq),
        grid=(batch, nb),
        in_specs=[
            pl.BlockSpec(memory_space=pltpu.SMEM),
            pl.BlockSpec(memory_space=pltpu.SMEM),
            pl.BlockSpec((BLK, A_NQ), lambda b, n: (b * nb + n, 0)),
            pl.BlockSpec((BLK, 2 * A_NK), lambda b, n: (b * nb + jnp.maximum(n - 1, 0), kv_col)),
            pl.BlockSpec((BLK, 2 * A_NK), lambda b, n: (b * nb + n, kv_col)),
            pl.BlockSpec((BLK, 2 * A_NK), lambda b, n: (b * nb + jnp.minimum(n + 1, nb - 1), kv_col)),
        ],
        out_specs=pl.BlockSpec((BLK, A_NQ), lambda b, n: (b * nb + n, 0)),
        out_shape=jax.ShapeDtypeStruct((batch * seq, A_NQ), BF16),
        compiler_params=_params("parallel", "parallel"),
        name="attn_a",
    )(slopes, sink, qkv, qkv, qkv, qkv)


def _attn_b_kernel(slope_ref, lam_ref, sub_ref, q_ref, k_ref, v_ref, o_ref, *, lambda_init):
    h = pl.program_id(1)
    i = pl.program_id(2)
    tq = q_ref.shape[0]
    seq = k_ref.shape[0]
    lv = lam_ref[...]
    lam = (jnp.exp(jnp.sum(lv[0:1] * lv[1:2], axis=-1, keepdims=True))
           - jnp.exp(jnp.sum(lv[2:3] * lv[3:4], axis=-1, keepdims=True)) + lambda_init)
    t = lax.broadcasted_iota(jnp.int32, (tq, seq), 0) + i * tq
    s_pos = lax.broadcasted_iota(jnp.int32, (tq, seq), 1)
    bias = slope_ref[h] * jnp.abs(t - s_pos).astype(F32)
    q = q_ref[...]
    k = k_ref[...]

    def softmax_map(c):
        s = _dot_nt(q[:, c * B_DH:(c + 1) * B_DH], k[:, c * B_DH:(c + 1) * B_DH]) - bias
        p = jnp.exp(s - jnp.max(s, axis=-1, keepdims=True))
        return p * (1.0 / jnp.sum(p, axis=-1, keepdims=True))

    wts = softmax_map(0) - lam * softmax_map(1)
    o = _dot(wts.astype(BF16), v_ref[...])
    o_ref[...] = (_row_rms(o, sub_ref[...]) * (1.0 - lambda_init)).astype(o_ref.dtype)


def _attn_b(qkv, slopes, lam_vecs, subln, batch, seq, lambda_init):
    tq = ATTN_B_TQ
    nq = seq // tq
    e = 2 * B_DH
    return pl.pallas_call(
        functools.partial(_attn_b_kernel, lambda_init=lambda_init),
        grid=(batch, B_H, nq),
        in_specs=[
            pl.BlockSpec(memory_space=pltpu.SMEM),
            pl.BlockSpec((4, B_DH), lambda b, h, i: (0, 0)),
            pl.BlockSpec((1, e), lambda b, h, i: (0, 0)),
            pl.BlockSpec((tq, e), lambda b, h, i: (b * nq + i, h)),
            pl.BlockSpec((seq, e), lambda b, h, i: (b, B_H + h)),
            pl.BlockSpec((seq, e), lambda b, h, i: (b, 2 * B_H + h)),
        ],
        out_specs=pl.BlockSpec((tq, e), lambda b, h, i: (b * nq + i, h)),
        out_shape=jax.ShapeDtypeStruct((batch * seq, B_W), BF16),
        compiler_params=_params("parallel", "parallel", "parallel"),
        name="attn_b",
    )(slopes, lam_vecs, subln.reshape(1, e), qkv, qkv, qkv)


def _proj_res_kernel(a_ref, w_ref, r_ref, o_ref):
    o_ref[...] = r_ref[...] + _dot(a_ref[...], w_ref[...])


def _proj_res(a, w, res):
    m, kdim = a.shape
    n = w.shape[1]
    tm, tn = PROJ_TM, PROJ_TN
    return pl.pallas_call(
        _proj_res_kernel,
        grid=(m // tm, n // tn),
        in_specs=[
            pl.BlockSpec((tm, kdim), lambda i, j: (i, 0)),
            pl.BlockSpec((kdim, tn), lambda i, j: (0, j)),
            pl.BlockSpec((tm, tn), lambda i, j: (i, j)),
        ],
        out_specs=pl.BlockSpec((tm, tn), lambda i, j: (i, j)),
        out_shape=jax.ShapeDtypeStruct((m, n), F32),
        compiler_params=_params("parallel", "parallel"),
        name="proj_res",
    )(a, w, res)


def _ffn_kernel(x_ref, g_ref, wg_ref, wu_ref, wo_ref, o_ref, hn_ref):
    f = pl.program_id(1)

    @pl.when(f == 0)
    def _():
        x = x_ref[...]
        hn_ref[...] = _row_rms(x, g_ref[...]).astype(BF16)
        o_ref[...] = x

    hn = hn_ref[...]
    gate = _dot(hn, wg_ref[...])
    up = _dot(hn, wu_ref[...])
    act = (gate * jax.nn.sigmoid(gate) * up).astype(BF16)
    o_ref[...] += _dot(act, wo_ref[...])


def _ffn(x, gain, w_in, w_out):
    m, d = x.shape
    tm, tf = FFN_TM, FFN_TF
    nf = D_FF // tf
    return pl.pallas_call(
        _ffn_kernel,
        grid=(m // tm, nf),
        in_specs=[
            pl.BlockSpec((tm, d), lambda i, f: (i, 0)),
            pl.BlockSpec((1, d), lambda i, f: (0, 0)),
            pl.BlockSpec((d, tf), lambda i, f: (0, f)),
            pl.BlockSpec((d, tf), lambda i, f: (0, nf + f)),
            pl.BlockSpec((tf, d), lambda i, f: (f, 0)),
        ],
        out_specs=pl.BlockSpec((tm, d), lambda i, f: (i, 0)),
        out_shape=jax.ShapeDtypeStruct((m, d), F32),
        scratch_shapes=[pltpu.VMEM((tm, d), BF16)],
        compiler_params=_params("parallel", "arbitrary"),
        name="ffn",
    )(x, gain.reshape(1, d), w_in, w_in, w_out)


def _ple_kernel(h_ref, p_ref, gn_ref, wg_ref, wp_ref, pn_ref, o_ref):
    h = h_ref[...]
    hn = _row_rms(h, gn_ref[...]).astype(BF16)
    gate = jax.nn.sigmoid(_dot(hn, wg_ref[...]))
    pe = _row_rms(_dot(p_ref[...].astype(BF16), wp_ref[...]), pn_ref[...])
    o_ref[...] = h + pe * gate


def _ple(h, p, gate_norm, w_gate, w_proj, post_norm):
    m, d = h.shape
    tm = PLE_TM
    return pl.pallas_call(
        _ple_kernel,
        grid=(m // tm,),
        in_specs=[
            pl.BlockSpec((tm, d), lambda i: (i, 0)),
            pl.BlockSpec((tm, PLE_DIM), lambda i: (i, 0)),
            pl.BlockSpec((1, d), lambda i: (0, 0)),
            pl.BlockSpec((d, d), lambda i: (0, 0)),
            pl.BlockSpec((PLE_DIM, d), lambda i: (0, 0)),
            pl.BlockSpec((1, d), lambda i: (0, 0)),
        ],
        out_specs=pl.BlockSpec((tm, d), lambda i: (i, 0)),
        out_shape=jax.ShapeDtypeStruct((m, d), F32),
        compiler_params=_params("parallel"),
        name="ple",
    )(h, p, gate_norm.reshape(1, d), w_gate, w_proj, post_norm.reshape(1, d))


def _alibi_slopes(n):
    return jnp.exp2(-8.0 * jnp.arange(1, n + 1, dtype=F32) / n)


def kernel(x, p, attn_norm, ffn_norm, a_w_qkv, a_q_norm, a_k_norm, a_sink, a_w_o, b_w_qkv, b_q_norm, b_k_norm, b_lambda, b_subln, b_w_o, w_ffn_in, w_ffn_out, ple_w_proj, ple_post_norm, ple_gate_norm, ple_w_gate):
    batch, seq, d = x.shape
    m = batch * seq
    h = x.reshape(m, d)
    p2 = p.reshape(DEPTH, m, PLE_DIM)
    for i in range(DEPTH):
        j = i // 2
        if i % 2 == 0:
            col_gain = jnp.concatenate([jnp.tile(a_q_norm[j] * (A_DH ** -0.5), A_HQ),
                                        jnp.tile(a_k_norm[j], A_KV), jnp.ones((A_NK,), F32)])
            col_mask = jnp.concatenate([jnp.ones((A_NQ + A_NK,), F32), jnp.zeros((A_NK,), F32)])
            qkv = _qkv_proj(h, attn_norm[i], a_w_qkv[j].astype(BF16), col_gain, col_mask,
                            A_DH, A_QKV // QKV_TN)
            mix = _attn_a(qkv, _alibi_slopes(A_HQ), a_sink[j].astype(F32), batch, seq)
            w_o = a_w_o[j]
        else:
            lambda_init = 0.8 - 0.6 * math.exp(-0.3 * i)
            col_gain = jnp.concatenate([jnp.tile(b_q_norm[j] * (B_DH ** -0.5), 2 * B_H),
                                        jnp.tile(b_k_norm[j], 2 * B_H), jnp.ones((B_W,), F32)])
            col_mask = jnp.concatenate([jnp.ones((2 * B_W,), F32), jnp.zeros((B_W,), F32)])
            qkv = _qkv_proj(h, attn_norm[i], b_w_qkv[j].astype(BF16), col_gain, col_mask,
                            B_DH, 2 * B_W // QKV_TN)
            mix = _attn_b(qkv, _alibi_slopes(B_H), b_lambda[j], b_subln[j], batch, seq, lambda_init)
            w_o = b_w_o[j]
        h = _proj_res(mix, w_o.astype(BF16), h)
        h = _ffn(h, ffn_norm[i], w_ffn_in[i].astype(BF16), w_ffn_out[i].astype(BF16))
        h = _ple(h, p2[i], ple_gate_norm[i], ple_w_gate[i].astype(BF16),
                 ple_w_proj[i].astype(BF16), ple_post_norm[i])
    return h.reshape(batch, seq, d)
```

```python
import functools
import math

import jax
import jax.numpy as jnp
from jax import lax
from jax.experimental import pallas as pl
from jax.experimental.pallas import tpu as pltpu

D_MODEL = 2048
DEPTH = 2
BLK = 128
WIN = 128
A_DH = 64
A_HQ = D_MODEL // A_DH
A_KV = 4
A_G = A_HQ // A_KV
A_NQ = A_HQ * A_DH
A_NK = A_KV * A_DH
A_QKV = A_NQ + 2 * A_NK
B_DH = 128
B_H = D_MODEL // (2 * B_DH)
B_W = B_H * 2 * B_DH
B_QKV = 3 * B_W
D_FF = int(math.ceil(8 * D_MODEL / 3 / 256) * 256)
PLE_DIM = 256
EPS = 1e-6
MASKED_DIST = 1e30
LOG2E = math.log2(math.e)

V7X_VMEM_LIMIT = 60 * 1024 * 1024
LANES = 128
MXU_DIM = 256
QKV_TM = 1024
QKV_A_TN, QKV_A_SUB = A_QKV // 2, 256
QKV_B_TN, QKV_B_SUB = B_W, 512
PROJ_TM, PROJ_TN = 1024, 1024
FFN_TM, FFN_TF = 1024, 512
PLE_TM = 512
ATTN_B_TQ = 256

F32 = jnp.float32
BF16 = jnp.bfloat16


def _params(*sem):
    return pltpu.CompilerParams(dimension_semantics=sem, vmem_limit_bytes=V7X_VMEM_LIMIT)


def _row_rms(x, gain):
    return x * lax.rsqrt(jnp.mean(x * x, axis=-1, keepdims=True) + EPS) * gain


def _dot(a, b):
    return jnp.dot(a, b, preferred_element_type=F32)


def _dot_nt(a, b):
    return lax.dot_general(a, b, (((1,), (1,)), ((), ())), preferred_element_type=F32)


def _qkv_kernel(x_ref, g_ref, w_ref, cg_ref, cm_ref, bd_ref, o_ref, hn_ref, *, group, n_norm, n_steps, sub):
    j = pl.program_id(1)
    tn = o_ref.shape[1]

    @pl.when(j == 0)
    def _():
        hn_ref[...] = _row_rms(x_ref[...], g_ref[...]).astype(BF16)

    def tiles(normed):
        for c in range(tn // sub):
            cols = slice(c * sub, (c + 1) * sub)
            o = _dot(hn_ref[...], w_ref[:, cols])
            if normed:
                sq = (o * o).astype(BF16)
                ss = jnp.concatenate(
                    [_dot(sq[:, t * MXU_DIM:(t + 1) * MXU_DIM], bd_ref[...]) for t in range(sub // MXU_DIM)],
                    axis=1)
                nrm = o * lax.rsqrt(ss * (1.0 / group) + EPS) * cg_ref[:, cols]
                o = jnp.where(cm_ref[:, cols] > 0.0, nrm, o)
            o_ref[:, cols] = o.astype(o_ref.dtype)

    pl.when(j < n_norm)(lambda: tiles(True))
    if n_norm < n_steps:
        pl.when(j >= n_norm)(lambda: tiles(False))


def _qkv_proj(x, gain, w_all, layer, col_gain, col_mask, group, tn, sub, n_norm):
    m, d = x.shape
    n = w_all.shape[2]
    tm = QKV_TM
    r = jnp.arange(MXU_DIM) // group
    bd = (r[:, None] == r[None, :]).astype(BF16)
    return pl.pallas_call(
        functools.partial(_qkv_kernel, group=group, n_norm=n_norm, n_steps=n // tn, sub=sub),
        grid=(m // tm, n // tn),
        in_specs=[
            pl.BlockSpec((tm, d), lambda i, j: (i, 0)),
            pl.BlockSpec((1, d), lambda i, j: (0, 0)),
            pl.BlockSpec((None, d, tn), lambda i, j: (layer, 0, j)),
            pl.BlockSpec((1, tn), lambda i, j: (0, j)),
            pl.BlockSpec((1, tn), lambda i, j: (0, j)),
            pl.BlockSpec((MXU_DIM, MXU_DIM), lambda i, j: (0, 0)),
        ],
        out_specs=pl.BlockSpec((tm, tn), lambda i, j: (i, j)),
        out_shape=jax.ShapeDtypeStruct((m, n), BF16),
        scratch_shapes=[pltpu.VMEM((tm, d), BF16)],
        compiler_params=_params("parallel", "arbitrary"),
        name="qkv_proj",
    )(x, gain.reshape(1, d), w_all, col_gain.reshape(1, n), col_mask.reshape(1, n), bd)


def _attn_a_kernel(slope_ref, sink_ref, hm_ref, q_ref, kvp_ref, kvc_ref, kvn_ref, o_ref, *, seq):
    n = pl.program_id(1)
    span = BLK + 2 * WIN
    pairs = A_G // 2
    kv = jnp.concatenate([kvp_ref[...], kvc_ref[...], kvn_ref[...]], axis=0)
    ti = lax.broadcasted_iota(jnp.int32, (BLK, span), 0)
    sj = lax.broadcasted_iota(jnp.int32, (BLK, span), 1)
    dist_i = jnp.abs(ti + WIN - sj)
    pos = sj + (n * BLK - WIN)
    valid = (dist_i <= WIN) & (pos >= 0) & (pos < seq)
    dist = jnp.where(valid, dist_i.astype(F32), MASKED_DIST)
    low_half = lax.broadcasted_iota(jnp.int32, (BLK, LANES), 1) < A_DH
    keep_lo = hm_ref[0:1, :]
    keep_hi = hm_ref[1:2, :]

    def block_diag(tile, head_in_low_half):
        swapped = jnp.concatenate([tile[:, A_DH:], tile[:, :A_DH]], axis=1)
        lo, hi = (tile, swapped) if head_in_low_half else (swapped, tile)
        return jnp.concatenate([lo * keep_lo, hi * keep_hi], axis=0)

    for kvh in range(A_KV):
        t0 = (kvh // 2) * LANES
        kbd = block_diag(kv[:, t0:t0 + LANES], kvh % 2 == 0)
        vbd = block_diag(kv[:, A_NK + t0:A_NK + t0 + LANES], kvh % 2 == 0)
        q0 = kvh * pairs * LANES
        qp = jnp.concatenate([q_ref[:, q0 + pp * LANES:q0 + (pp + 1) * LANES] for pp in range(pairs)], axis=0)
        s = _dot_nt(qp, kbd)
        probs, scales = [], []
        for pp in range(pairs):
            halves, rls = [], []
            for e in range(2):
                h = kvh * A_G + 2 * pp + e
                sh = s[pp * BLK:(pp + 1) * BLK, e * span:(e + 1) * span] - slope_ref[h] * dist
                sink = sink_ref[h]
                mx = jnp.maximum(jnp.max(sh, axis=-1, keepdims=True), sink)
                p = jnp.exp2(sh - mx)
                l = jnp.sum(p, axis=-1, keepdims=True) + jnp.exp2(sink - mx)
                halves.append(p.astype(BF16))
                rls.append(1.0 / l)
            probs.append(jnp.concatenate(halves, axis=1))
            scales.append(jnp.where(low_half, rls[0], rls[1]))
        pv = _dot(jnp.concatenate(probs, axis=0), vbd)
        for pp in range(pairs):
            o_ref[:, q0 + pp * LANES:q0 + (pp + 1) * LANES] = (
                pv[pp * BLK:(pp + 1) * BLK] * scales[pp]).astype(o_ref.dtype)


def _attn_a(qkv, slopes, sink, batch, seq):
    nb = seq // BLK
    kv_col = A_NQ // (2 * A_NK)
    lane = jnp.arange(LANES)
    half_masks = jnp.stack([lane < A_DH, lane >= A_DH]).astype(BF16)
    return pl.pallas_call(
        functools.partial(_attn_a_kernel, seq=seq),
        grid=(batch, nb),
        in_specs=[
            pl.BlockSpec(memory_space=pltpu.SMEM),
            pl.BlockSpec(memory_space=pltpu.SMEM),
            pl.BlockSpec((2, LANES), lambda b, n: (0, 0)),
            pl.BlockSpec((BLK, A_NQ), lambda b, n: (b * nb + n, 0)),
            pl.BlockSpec((BLK, 2 * A_NK), lambda b, n: (b * nb + jnp.maximum(n - 1, 0), kv_col)),
            pl.BlockSpec((BLK, 2 * A_NK), lambda b, n: (b * nb + n, kv_col)),
            pl.BlockSpec((BLK, 2 * A_NK), lambda b, n: (b * nb + jnp.minimum(n + 1, nb - 1), kv_col)),
        ],
        out_specs=pl.BlockSpec((BLK, A_NQ), lambda b, n: (b * nb + n, 0)),
        out_shape=jax.ShapeDtypeStruct((batch * seq, A_NQ), BF16),
        compiler_params=_params("parallel", "parallel"),
        name="attn_a",
    )(slopes, sink, half_masks, qkv, qkv, qkv, qkv)


def _attn_b_kernel(slope_ref, lam_ref, sub_ref, q_ref, k_ref, v_ref, o_ref, bias_ref, *, lambda_init):
    h = pl.program_id(0)
    b = pl.program_id(1)
    i = pl.program_id(2)
    tq = q_ref.shape[0]
    seq = k_ref.shape[0]
    rows = pl.ds(pl.multiple_of(i * tq, tq), tq)

    @pl.when(b == 0)
    def _():
        t = (lax.broadcasted_iota(jnp.int32, (tq, seq), 0) + i * tq).astype(F32)
        s_pos = lax.broadcasted_iota(jnp.int32, (tq, seq), 1).astype(F32)
        bias_ref[rows, :] = slope_ref[h] * jnp.abs(t - s_pos)

    lv = lam_ref[...]
    lam = (jnp.exp(jnp.sum(lv[0:1] * lv[1:2], axis=-1, keepdims=True))
           - jnp.exp(jnp.sum(lv[2:3] * lv[3:4], axis=-1, keepdims=True)) + lambda_init)
    v = v_ref[...]

    def softmax_pv(c):
        s = _dot_nt(q_ref[:, c * B_DH:(c + 1) * B_DH], k_ref[:, c * B_DH:(c + 1) * B_DH]) - bias_ref[rows, :]
        p = jnp.exp2(s - jnp.max(s, axis=-1, keepdims=True))
        return _dot(p.astype(BF16), v), jnp.sum(p, axis=-1, keepdims=True)

    o1, l1 = softmax_pv(0)
    o2, l2 = softmax_pv(1)
    o = o1 * (1.0 / l1) - o2 * (lam / l2)
    o_ref[...] = (_row_rms(o, sub_ref[...]) * (1.0 - lambda_init)).astype(o_ref.dtype)


def _attn_b(qkv, slopes, lam_vecs, subln, batch, seq, lambda_init):
    tq = ATTN_B_TQ
    nq = seq // tq
    e = 2 * B_DH
    return pl.pallas_call(
        functools.partial(_attn_b_kernel, lambda_init=lambda_init),
        grid=(B_H, batch, nq),
        in_specs=[
            pl.BlockSpec(memory_space=pltpu.SMEM),
            pl.BlockSpec((4, B_DH), lambda h, b, i: (0, 0)),
            pl.BlockSpec((1, e), lambda h, b, i: (0, 0)),
            pl.BlockSpec((tq, e), lambda h, b, i: (b * nq + i, h)),
            pl.BlockSpec((seq, e), lambda h, b, i: (b, B_H + h)),
            pl.BlockSpec((seq, e), lambda h, b, i: (b, 2 * B_H + h)),
        ],
        out_specs=pl.BlockSpec((tq, e), lambda h, b, i: (b * nq + i, h)),
        out_shape=jax.ShapeDtypeStruct((batch * seq, B_W), BF16),
        scratch_shapes=[pltpu.VMEM((seq, seq), F32)],
        compiler_params=_params("parallel", "arbitrary", "arbitrary"),
        name="attn_b",
    )(slopes, lam_vecs, subln.reshape(1, e), qkv, qkv, qkv)


def _proj_res_kernel(a_ref, w_ref, r_ref, o_ref):
    o_ref[...] = r_ref[...] + _dot(a_ref[...], w_ref[...])


def _proj_res(a, w_all, layer, res):
    m, kdim = a.shape
    n = w_all.shape[2]
    tm, tn = PROJ_TM, PROJ_TN
    return pl.pallas_call(
        _proj_res_kernel,
        grid=(m // tm, n // tn),
        in_specs=[
            pl.BlockSpec((tm, kdim), lambda i, j: (i, 0)),
            pl.BlockSpec((None, kdim, tn), lambda i, j: (layer, 0, j)),
            pl.BlockSpec((tm, tn), lambda i, j: (i, j)),
        ],
        out_specs=pl.BlockSpec((tm, tn), lambda i, j: (i, j)),
        out_shape=jax.ShapeDtypeStruct((m, n), F32),
        compiler_params=_params("parallel", "parallel"),
        name="proj_res",
    )(a, w_all, res)


def _ffn_kernel(x_ref, g_ref, wg_ref, wu_ref, wo_ref, o_ref, hn_ref):
    f = pl.program_id(1)

    @pl.when(f == 0)
    def _():
        x = x_ref[...]
        hn_ref[...] = _row_rms(x, g_ref[...]).astype(BF16)
        o_ref[...] = x

    hn = hn_ref[...]
    gate = _dot(hn, wg_ref[...])
    up = _dot(hn, wu_ref[...])
    act = (gate * jax.nn.sigmoid(gate) * up).astype(BF16)
    o_ref[...] += _dot(act, wo_ref[...])


def _ffn(x, gain, w_in_all, w_out_all, layer):
    m, d = x.shape
    tm, tf = FFN_TM, FFN_TF
    nf = D_FF // tf
    return pl.pallas_call(
        _ffn_kernel,
        grid=(m // tm, nf),
        in_specs=[
            pl.BlockSpec((tm, d), lambda i, f: (i, 0)),
            pl.BlockSpec((1, d), lambda i, f: (0, 0)),
            pl.BlockSpec((None, d, tf), lambda i, f: (layer, 0, f)),
            pl.BlockSpec((None, d, tf), lambda i, f: (layer, 0, nf + f)),
            pl.BlockSpec((None, tf, d), lambda i, f: (layer, f, 0)),
        ],
        out_specs=pl.BlockSpec((tm, d), lambda i, f: (i, 0)),
        out_shape=jax.ShapeDtypeStruct((m, d), F32),
        scratch_shapes=[pltpu.VMEM((tm, d), BF16)],
        compiler_params=_params("parallel", "arbitrary"),
        name="ffn",
    )(x, gain.reshape(1, d), w_in_all, w_in_all, w_out_all)


def _ple_kernel(h_ref, p_ref, gn_ref, wg_ref, wp_ref, pn_ref, o_ref):
    h = h_ref[...]
    hn = _row_rms(h, gn_ref[...]).astype(BF16)
    gate = jax.nn.sigmoid(_dot(hn, wg_ref[...]))
    pe = _row_rms(_dot(p_ref[...].astype(BF16), wp_ref[...]), pn_ref[...])
    o_ref[...] = h + pe * gate


def _ple(h, p_all, gate_norm, w_gate_all, w_proj_all, post_norm, layer):
    m, d = h.shape
    tm = PLE_TM
    return pl.pallas_call(
        _ple_kernel,
        grid=(m // tm,),
        in_specs=[
            pl.BlockSpec((tm, d), lambda i: (i, 0)),
            pl.BlockSpec((None, tm, PLE_DIM), lambda i: (layer, i, 0)),
            pl.BlockSpec((1, d), lambda i: (0, 0)),
            pl.BlockSpec((None, d, d), lambda i: (layer, 0, 0)),
            pl.BlockSpec((None, PLE_DIM, d), lambda i: (layer, 0, 0)),
            pl.BlockSpec((1, d), lambda i: (0, 0)),
        ],
        out_specs=pl.BlockSpec((tm, d), lambda i: (i, 0)),
        out_shape=jax.ShapeDtypeStruct((m, d), F32),
        compiler_params=_params("parallel"),
        name="ple",
    )(h, p_all, gate_norm.reshape(1, d), w_gate_all, w_proj_all, post_norm.reshape(1, d))


def _alibi_slopes(n):
    return jnp.exp2(-8.0 * jnp.arange(1, n + 1, dtype=F32) / n)


def kernel(x, p, attn_norm, ffn_norm, a_w_qkv, a_q_norm, a_k_norm, a_sink, a_w_o, b_w_qkv, b_q_norm, b_k_norm, b_lambda, b_subln, b_w_o, w_ffn_in, w_ffn_out, ple_w_proj, ple_post_norm, ple_gate_norm, ple_w_gate):
    batch, seq, d = x.shape
    m = batch * seq
    h = x.reshape(m, d)
    p_all = p.reshape(DEPTH, m, PLE_DIM)
    a_w_qkv, a_w_o, b_w_qkv, b_w_o, w_ffn_in, w_ffn_out, ple_w_proj, ple_w_gate = (
        w.astype(BF16) for w in (a_w_qkv, a_w_o, b_w_qkv, b_w_o, w_ffn_in, w_ffn_out, ple_w_proj, ple_w_gate))
    for i in range(DEPTH):
        j = i // 2
        if i % 2 == 0:
            col_gain = jnp.concatenate([jnp.tile(a_q_norm[j] * (A_DH ** -0.5 * LOG2E), A_HQ),
                                        jnp.tile(a_k_norm[j], A_KV), jnp.ones((A_NK,), F32)])
            col_mask = jnp.concatenate([jnp.ones((A_NQ + A_NK,), F32), jnp.zeros((A_NK,), F32)])
            qkv = _qkv_proj(h, attn_norm[i], a_w_qkv, j, col_gain, col_mask, A_DH,
                            QKV_A_TN, QKV_A_SUB, A_QKV // QKV_A_TN)
            mix = _attn_a(qkv, _alibi_slopes(A_HQ) * LOG2E, a_sink[j].astype(F32) * LOG2E, batch, seq)
            h = _proj_res(mix, a_w_o, j, h)
        else:
            lambda_init = 0.8 - 0.6 * math.exp(-0.3 * i)
            col_gain = jnp.concatenate([jnp.tile(b_q_norm[j] * (B_DH ** -0.5 * LOG2E), 2 * B_H),
                                        jnp.tile(b_k_norm[j], 2 * B_H), jnp.ones((B_W,), F32)])
            col_mask = jnp.concatenate([jnp.ones((2 * B_W,), F32), jnp.zeros((B_W,), F32)])
            qkv = _qkv_proj(h, attn_norm[i], b_w_qkv, j, col_gain, col_mask, B_DH,
                            QKV_B_TN, QKV_B_SUB, 2 * B_W // QKV_B_TN)
            mix = _attn_b(qkv, _alibi_slopes(B_H) * LOG2E, b_lambda[j], b_subln[j], batch, seq, lambda_init)
            h = _proj_res(mix, b_w_o, j, h)
        h = _ffn(h, ffn_norm[i], w_ffn_in, w_ffn_out, i)
        h = _ple(h, p_all, ple_gate_norm[i], ple_w_gate, ple_w_proj, ple_post_norm[i], i)
    return h.reshape(batch, seq, d)
```

```python
import functools
import math

import jax
import jax.numpy as jnp
from jax import lax
from jax.experimental import pallas as pl
from jax.experimental.pallas import tpu as pltpu

D_MODEL = 2048
DEPTH = 2
BLK = 128
WIN = 128
A_DH = 64
A_HQ = D_MODEL // A_DH
A_KV = 4
A_G = A_HQ // A_KV
A_NQ = A_HQ * A_DH
A_NK = A_KV * A_DH
A_QKV = A_NQ + 2 * A_NK
B_DH = 128
B_H = D_MODEL // (2 * B_DH)
B_W = B_H * 2 * B_DH
B_QKV = 3 * B_W
D_FF = int(math.ceil(8 * D_MODEL / 3 / 256) * 256)
PLE_DIM = 256
EPS = 1e-6
MASKED_DIST = 1e30
LOG2E = math.log2(math.e)

V7X_VMEM_LIMIT = 60 * 1024 * 1024
LANES = 128
MXU_DIM = 256
QKV_A_TM, QKV_A_TN, QKV_A_SUB = 512, A_QKV, 512
QKV_B_TM, QKV_B_TN, QKV_B_SUB = 1024, B_W, 512
PROJ_TM, PROJ_TN = 1024, 1024
FFN_TM, FFN_TF = 1024, 512
PLE_TM, PLE_SUB = 512, 512
ATTN_B_TQ, ATTN_B_KCHUNK = 256, 2 * LANES

F32 = jnp.float32
BF16 = jnp.bfloat16


def _params(*sem, flags=None):
    return pltpu.CompilerParams(dimension_semantics=sem, vmem_limit_bytes=V7X_VMEM_LIMIT, flags=flags)


def _row_rms(x, gain):
    return x * lax.rsqrt(jnp.mean(x * x, axis=-1, keepdims=True) + EPS) * gain


def _dot(a, b):
    return jnp.dot(a, b, preferred_element_type=F32)


def _dot_nt(a, b):
    return lax.dot_general(a, b, (((1,), (1,)), ((), ())), preferred_element_type=F32)


def _qkv_kernel(x_ref, g_ref, w_ref, cg_ref, cm_ref, bd_ref, o_ref, hn_ref, *, group, n_norm, n_steps, sub):
    j = pl.program_id(1)
    tn = o_ref.shape[1]

    @pl.when(j == 0)
    def _():
        hn_ref[...] = _row_rms(x_ref[...], g_ref[...]).astype(BF16)

    def tiles(normed):
        for c in range(tn // sub):
            cols = slice(c * sub, (c + 1) * sub)
            o = _dot(hn_ref[...], w_ref[:, cols])
            if normed:
                sq = (o * o).astype(BF16)
                ss = jnp.concatenate(
                    [_dot(sq[:, t * MXU_DIM:(t + 1) * MXU_DIM], bd_ref[...]) for t in range(sub // MXU_DIM)],
                    axis=1)
                nrm = o * lax.rsqrt(ss * (1.0 / group) + EPS) * cg_ref[:, cols]
                o = jnp.where(cm_ref[:, cols] > 0.0, nrm, o)
            o_ref[:, cols] = o.astype(o_ref.dtype)

    pl.when(j < n_norm)(lambda: tiles(True))
    if n_norm < n_steps:
        pl.when(j >= n_norm)(lambda: tiles(False))


def _qkv_proj(x, gain, w_all, layer, col_gain, col_mask, group, tm, tn, sub, n_norm):
    m, d = x.shape
    n = w_all.shape[2]
    r = jnp.arange(MXU_DIM) // group
    bd = (r[:, None] == r[None, :]).astype(BF16)
    return pl.pallas_call(
        functools.partial(_qkv_kernel, group=group, n_norm=n_norm, n_steps=n // tn, sub=sub),
        grid=(m // tm, n // tn),
        in_specs=[
            pl.BlockSpec((tm, d), lambda i, j: (i, 0)),
            pl.BlockSpec((1, d), lambda i, j: (0, 0)),
            pl.BlockSpec((None, d, tn), lambda i, j: (layer, 0, j)),
            pl.BlockSpec((1, tn), lambda i, j: (0, j)),
            pl.BlockSpec((1, tn), lambda i, j: (0, j)),
            pl.BlockSpec((MXU_DIM, MXU_DIM), lambda i, j: (0, 0)),
        ],
        out_specs=pl.BlockSpec((tm, tn), lambda i, j: (i, j)),
        out_shape=jax.ShapeDtypeStruct((m, n), BF16),
        scratch_shapes=[pltpu.VMEM((tm, d), BF16)],
        compiler_params=_params("parallel", "arbitrary"),
        name="qkv_proj",
    )(x, gain.reshape(1, d), w_all, col_gain.reshape(1, n), col_mask.reshape(1, n), bd)


def _attn_a_kernel(slope_ref, sink_ref, hm_ref, q_ref, kvp_ref, kvc_ref, kvn_ref, o_ref, *, seq):
    n = pl.program_id(1)
    span = BLK + 2 * WIN
    pairs = A_G // 2
    kv = jnp.concatenate([kvp_ref[...], kvc_ref[...], kvn_ref[...]], axis=0)
    ti = lax.broadcasted_iota(jnp.int32, (BLK, span), 0)
    sj = lax.broadcasted_iota(jnp.int32, (BLK, span), 1)
    dist_i = jnp.abs(ti + WIN - sj)
    pos = sj + (n * BLK - WIN)
    valid = (dist_i <= WIN) & (pos >= 0) & (pos < seq)
    dist = jnp.where(valid, dist_i.astype(F32), MASKED_DIST)
    low_half = lax.broadcasted_iota(jnp.int32, (BLK, LANES), 1) < A_DH
    keep_lo = hm_ref[0:1, :]
    keep_hi = hm_ref[1:2, :]

    def block_diag(tile, head_in_low_half):
        swapped = jnp.concatenate([tile[:, A_DH:], tile[:, :A_DH]], axis=1)
        lo, hi = (tile, swapped) if head_in_low_half else (swapped, tile)
        return jnp.concatenate([lo * keep_lo, hi * keep_hi], axis=0)

    for kvh in range(A_KV):
        t0 = (kvh // 2) * LANES
        kbd = block_diag(kv[:, t0:t0 + LANES], kvh % 2 == 0)
        vbd = block_diag(kv[:, A_NK + t0:A_NK + t0 + LANES], kvh % 2 == 0)
        q0 = kvh * pairs * LANES
        qp = jnp.concatenate([q_ref[:, q0 + pp * LANES:q0 + (pp + 1) * LANES] for pp in range(pairs)], axis=0)
        s = _dot_nt(qp, kbd)
        probs, scales = [], []
        for pp in range(pairs):
            halves, rls = [], []
            for e in range(2):
                h = kvh * A_G + 2 * pp + e
                sh = s[pp * BLK:(pp + 1) * BLK, e * span:(e + 1) * span] - slope_ref[h] * dist
                sink = sink_ref[h]
                mx = jnp.maximum(jnp.max(sh, axis=-1, keepdims=True), sink)
                p = jnp.exp2(sh - mx)
                l = jnp.sum(p, axis=-1, keepdims=True) + jnp.exp2(sink - mx)
                halves.append(p.astype(BF16))
                rls.append(1.0 / l)
            probs.append(jnp.concatenate(halves, axis=1))
            scales.append(jnp.where(low_half, rls[0], rls[1]))
        pv = _dot(jnp.concatenate(probs, axis=0), vbd)
        for pp in range(pairs):
            o_ref[:, q0 + pp * LANES:q0 + (pp + 1) * LANES] = (
                pv[pp * BLK:(pp + 1) * BLK] * scales[pp]).astype(o_ref.dtype)


def _attn_a(qkv, slopes, sink, batch, seq):
    nb = seq // BLK
    kv_col = A_NQ // (2 * A_NK)
    lane = jnp.arange(LANES)
    half_masks = jnp.stack([lane < A_DH, lane >= A_DH]).astype(BF16)
    return pl.pallas_call(
        functools.partial(_attn_a_kernel, seq=seq),
        grid=(batch, nb),
        in_specs=[
            pl.BlockSpec(memory_space=pltpu.SMEM),
            pl.BlockSpec(memory_space=pltpu.SMEM),
            pl.BlockSpec((2, LANES), lambda b, n: (0, 0)),
            pl.BlockSpec((BLK, A_NQ), lambda b, n: (b * nb + n, 0)),
            pl.BlockSpec((BLK, 2 * A_NK), lambda b, n: (b * nb + jnp.maximum(n - 1, 0), kv_col)),
            pl.BlockSpec((BLK, 2 * A_NK), lambda b, n: (b * nb + n, kv_col)),
            pl.BlockSpec((BLK, 2 * A_NK), lambda b, n: (b * nb + jnp.minimum(n + 1, nb - 1), kv_col)),
        ],
        out_specs=pl.BlockSpec((BLK, A_NQ), lambda b, n: (b * nb + n, 0)),
        out_shape=jax.ShapeDtypeStruct((batch * seq, A_NQ), BF16),
        compiler_params=_params("parallel", "parallel"),
        name="attn_a",
    )(slopes, sink, half_masks, qkv, qkv, qkv, qkv)


ATTN_B_LAG = 1


def _attn_b_block(t, batch, nq):
    t = jnp.clip(t, 0, B_H * batch * nq - 1)
    return t // (nq * batch), (t // nq) % batch, t % nq


def _attn_b_kernel(slope_ref, lam_ref, sub_ref, q_ref, k_ref, v_ref, o_ref,
                   bias_ref, s0_ref, s1_ref, m0_ref, m1_ref, *, lambda_init, batch, nq):
    t = pl.program_id(0)
    tq = q_ref.shape[0]
    seq = k_ref.shape[0]
    h, b, i = _attn_b_block(t, batch, nq)
    rows = pl.ds(pl.multiple_of(i * tq, tq), tq)

    @pl.when(t == 0)
    def _():
        s1_ref[...] = jnp.zeros_like(s1_ref)
        m1_ref[...] = jnp.zeros_like(m1_ref)

    @pl.when(b == 0)
    def _():
        tpos = (lax.broadcasted_iota(jnp.int32, (tq, seq), 0) + i * tq).astype(F32)
        spos = lax.broadcasted_iota(jnp.int32, (tq, seq), 1).astype(F32)
        bias_ref[rows, :] = slope_ref[h] * jnp.abs(tpos - spos)

    lv = lam_ref[...]
    lam = (jnp.exp(jnp.sum(lv[0:1] * lv[1:2], axis=-1, keepdims=True))
           - jnp.exp(jnp.sum(lv[2:3] * lv[3:4], axis=-1, keepdims=True)) + lambda_init)

    def stages(s_w, m_w, s_r, m_r):
        pv = [None, None]
        l_part = [None, None]
        m_run = [None, None]
        for n in range(seq // ATTN_B_KCHUNK):
            k0 = n * ATTN_B_KCHUNK
            keys = slice(k0, k0 + ATTN_B_KCHUNK)
            for c in range(2):
                m_rep = m_r[c]
                p_lo = jnp.exp2(s_r[c, :, k0:k0 + LANES] - m_rep)
                p_hi = jnp.exp2(s_r[c, :, k0 + LANES:k0 + 2 * LANES] - m_rep)
                l_new = p_lo + p_hi
                l_part[c] = l_new if n == 0 else l_part[c] + l_new
                pv_new = _dot(jnp.concatenate([p_lo, p_hi], axis=1).astype(BF16), v_ref[keys, :])
                pv[c] = pv_new if n == 0 else pv[c] + pv_new
            for c in range(2):
                dims = slice(c * B_DH, (c + 1) * B_DH)
                s = _dot_nt(q_ref[:, dims], k_ref[keys, dims]) - bias_ref[rows, keys]
                s_w[c, :, keys] = s
                part = jnp.maximum(s[:, :LANES], s[:, LANES:])
                m_run[c] = part if n == 0 else jnp.maximum(m_run[c], part)
        for c in range(2):
            m_w[c] = jnp.broadcast_to(jnp.max(m_run[c], axis=-1, keepdims=True), (tq, LANES))
        l1 = jnp.sum(l_part[0], axis=-1, keepdims=True)
        l2 = jnp.sum(l_part[1], axis=-1, keepdims=True)
        o = pv[0] * (1.0 / l1) - pv[1] * (lam / l2)
        o_ref[...] = (_row_rms(o, sub_ref[...]) * (1.0 - lambda_init)).astype(o_ref.dtype)

    odd = t % 2
    pl.when(odd == 0)(lambda: stages(s0_ref, m0_ref, s1_ref, m1_ref))
    pl.when(odd == 1)(lambda: stages(s1_ref, m1_ref, s0_ref, m0_ref))


def _attn_b(qkv, slopes, lam_vecs, subln, batch, seq, lambda_init):
    tq = ATTN_B_TQ
    nq = seq // tq
    e = 2 * B_DH

    def q_map(t):
        h, b, i = _attn_b_block(t, batch, nq)
        return b * nq + i, h

    def k_map(t):
        h, b, _ = _attn_b_block(t, batch, nq)
        return b, B_H + h

    def v_map(t):
        h, b, _ = _attn_b_block(t - ATTN_B_LAG, batch, nq)
        return b, 2 * B_H + h

    return pl.pallas_call(
        functools.partial(_attn_b_kernel, lambda_init=lambda_init, batch=batch, nq=nq),
        grid=(B_H * batch * nq + ATTN_B_LAG,),
        in_specs=[
            pl.BlockSpec(memory_space=pltpu.SMEM),
            pl.BlockSpec((4, B_DH), lambda t: (0, 0)),
            pl.BlockSpec((1, e), lambda t: (0, 0)),
            pl.BlockSpec((tq, e), q_map),
            pl.BlockSpec((seq, e), k_map),
            pl.BlockSpec((seq, e), v_map),
        ],
        out_specs=pl.BlockSpec((tq, e), lambda t: q_map(t - ATTN_B_LAG)),
        out_shape=jax.ShapeDtypeStruct((batch * seq, B_W), BF16),
        scratch_shapes=[pltpu.VMEM((seq, seq), F32),
                        pltpu.VMEM((2, tq, seq), F32), pltpu.VMEM((2, tq, seq), F32),
                        pltpu.VMEM((2, tq, LANES), F32), pltpu.VMEM((2, tq, LANES), F32)],
        compiler_params=_params("arbitrary"),
        name="attn_b",
    )(slopes, lam_vecs, subln.reshape(1, e), qkv, qkv, qkv)


def _proj_res_kernel(a_ref, w_ref, r_ref, o_ref):
    o_ref[...] = r_ref[...] + _dot(a_ref[...], w_ref[...])


def _proj_res(a, w_all, layer, res):
    m, kdim = a.shape
    n = w_all.shape[2]
    tm, tn = PROJ_TM, PROJ_TN
    return pl.pallas_call(
        _proj_res_kernel,
        grid=(m // tm, n // tn),
        in_specs=[
            pl.BlockSpec((tm, kdim), lambda i, j: (i, 0)),
            pl.BlockSpec((None, kdim, tn), lambda i, j: (layer, 0, j)),
            pl.BlockSpec((tm, tn), lambda i, j: (i, j)),
        ],
        out_specs=pl.BlockSpec((tm, tn), lambda i, j: (i, j)),
        out_shape=jax.ShapeDtypeStruct((m, n), F32),
        compiler_params=_params("parallel", "parallel"),
        name="proj_res",
    )(a, w_all, res)


def _ffn_kernel(x_ref, g_ref, wg_ref, wu_ref, wo_ref, o_ref, hn_ref):
    f = pl.program_id(1)

    @pl.when(f == 0)
    def _():
        x = x_ref[...]
        hn_ref[...] = _row_rms(x, g_ref[...]).astype(BF16)
        o_ref[...] = x

    hn = hn_ref[...]
    gate = _dot(hn, wg_ref[...])
    up = _dot(hn, wu_ref[...])
    act = (gate * jax.nn.sigmoid(gate) * up).astype(BF16)
    o_ref[...] += _dot(act, wo_ref[...])


def _ffn(x, gain, w_in_all, w_out_all, layer):
    m, d = x.shape
    tm, tf = FFN_TM, FFN_TF
    nf = D_FF // tf
    return pl.pallas_call(
        _ffn_kernel,
        grid=(m // tm, nf),
        in_specs=[
            pl.BlockSpec((tm, d), lambda i, f: (i, 0)),
            pl.BlockSpec((1, d), lambda i, f: (0, 0)),
            pl.BlockSpec((None, d, tf), lambda i, f: (layer, 0, f)),
            pl.BlockSpec((None, d, tf), lambda i, f: (layer, 0, nf + f)),
            pl.BlockSpec((None, tf, d), lambda i, f: (layer, f, 0)),
        ],
        out_specs=pl.BlockSpec((tm, d), lambda i, f: (i, 0)),
        out_shape=jax.ShapeDtypeStruct((m, d), F32),
        scratch_shapes=[pltpu.VMEM((tm, d), BF16)],
        compiler_params=_params("parallel", "arbitrary"),
        name="ffn",
    )(x, gain.reshape(1, d), w_in_all, w_in_all, w_out_all)


def _ple_kernel(h_ref, p_ref, gn_ref, wg_ref, wp_ref, pn_ref, o_ref):
    hn = _row_rms(h_ref[...], gn_ref[...]).astype(BF16)
    pe = _row_rms(_dot(p_ref[...].astype(BF16), wp_ref[...]), pn_ref[...])
    for c in range(o_ref.shape[1] // PLE_SUB):
        cols = slice(c * PLE_SUB, (c + 1) * PLE_SUB)
        gate = jax.nn.sigmoid(_dot(hn, wg_ref[:, cols]))
        o_ref[:, cols] = h_ref[:, cols] + pe[:, cols] * gate


def _ple(h, p_all, gate_norm, w_gate_all, w_proj_all, post_norm, layer):
    m, d = h.shape
    tm = PLE_TM
    return pl.pallas_call(
        _ple_kernel,
        grid=(m // tm,),
        in_specs=[
            pl.BlockSpec((tm, d), lambda i: (i, 0)),
            pl.BlockSpec((None, tm, PLE_DIM), lambda i: (layer, i, 0)),
            pl.BlockSpec((1, d), lambda i: (0, 0)),
            pl.BlockSpec((None, d, d), lambda i: (layer, 0, 0)),
            pl.BlockSpec((None, PLE_DIM, d), lambda i: (layer, 0, 0)),
            pl.BlockSpec((1, d), lambda i: (0, 0)),
        ],
        out_specs=pl.BlockSpec((tm, d), lambda i: (i, 0)),
        out_shape=jax.ShapeDtypeStruct((m, d), F32),
        compiler_params=_params("parallel"),
        name="ple",
    )(h, p_all, gate_norm.reshape(1, d), w_gate_all, w_proj_all, post_norm.reshape(1, d))


def _alibi_slopes(n):
    return jnp.exp2(-8.0 * jnp.arange(1, n + 1, dtype=F32) / n)


def kernel(x, p, attn_norm, ffn_norm, a_w_qkv, a_q_norm, a_k_norm, a_sink, a_w_o, b_w_qkv, b_q_norm, b_k_norm, b_lambda, b_subln, b_w_o, w_ffn_in, w_ffn_out, ple_w_proj, ple_post_norm, ple_gate_norm, ple_w_gate):
    batch, seq, d = x.shape
    m = batch * seq
    h = x.reshape(m, d)
    p_all = p.reshape(DEPTH, m, PLE_DIM)
    a_w_qkv, a_w_o, b_w_qkv, b_w_o, w_ffn_in, w_ffn_out, ple_w_proj, ple_w_gate = (
        w.astype(BF16) for w in (a_w_qkv, a_w_o, b_w_qkv, b_w_o, w_ffn_in, w_ffn_out, ple_w_proj, ple_w_gate))
    for i in range(DEPTH):
        j = i // 2
        if i % 2 == 0:
            col_gain = jnp.concatenate([jnp.tile(a_q_norm[j] * (A_DH ** -0.5 * LOG2E), A_HQ),
                                        jnp.tile(a_k_norm[j], A_KV), jnp.ones((A_NK,), F32)])
            col_mask = jnp.concatenate([jnp.ones((A_NQ + A_NK,), F32), jnp.zeros((A_NK,), F32)])
            qkv = _qkv_proj(h, attn_norm[i], a_w_qkv, j, col_gain, col_mask, A_DH,
                            QKV_A_TM, QKV_A_TN, QKV_A_SUB, A_QKV // QKV_A_TN)
            mix = _attn_a(qkv, _alibi_slopes(A_HQ) * LOG2E, a_sink[j].astype(F32) * LOG2E, batch, seq)
            h = _proj_res(mix, a_w_o, j, h)
        else:
            lambda_init = 0.8 - 0.6 * math.exp(-0.3 * i)
            col_gain = jnp.concatenate([jnp.tile(b_q_norm[j] * (B_DH ** -0.5 * LOG2E), 2 * B_H),
                                        jnp.tile(b_k_norm[j], 2 * B_H), jnp.ones((B_W,), F32)])
            col_mask = jnp.concatenate([jnp.ones((2 * B_W,), F32), jnp.zeros((B_W,), F32)])
            qkv = _qkv_proj(h, attn_norm[i], b_w_qkv, j, col_gain, col_mask, B_DH,
                            QKV_B_TM, QKV_B_TN, QKV_B_SUB, 2 * B_W // QKV_B_TN)
            mix = _attn_b(qkv, _alibi_slopes(B_H) * LOG2E, b_lambda[j], b_subln[j], batch, seq, lambda_init)
            h = _proj_res(mix, b_w_o, j, h)
        h = _ffn(h, ffn_norm[i], w_ffn_in, w_ffn_out, i)
        h = _ple(h, p_all, ple_gate_norm[i], ple_w_gate, ple_w_proj, ple_post_norm[i], i)
    return h.reshape(batch, seq, d)
```

```python
import functools
import math

import jax
import jax.numpy as jnp
from jax import lax
from jax.experimental import pallas as pl
from jax.experimental.pallas import tpu as pltpu

D_MODEL = 2048
DEPTH = 2
BLK = 128
WIN = 128
A_DH = 64
A_HQ = D_MODEL // A_DH
A_KV = 4
A_G = A_HQ // A_KV
A_NQ = A_HQ * A_DH
A_NK = A_KV * A_DH
A_QKV = A_NQ + 2 * A_NK
B_DH = 128
B_H = D_MODEL // (2 * B_DH)
B_W = B_H * 2 * B_DH
B_QKV = 3 * B_W
D_FF = int(math.ceil(8 * D_MODEL / 3 / 256) * 256)
PLE_DIM = 256
EPS = 1e-6
MASKED_DIST = 1e30
LOG2E = math.log2(math.e)

V7X_VMEM_LIMIT = 60 * 1024 * 1024
LANES = 128
MXU_DIM = 256
QKV_A_TM, QKV_A_TN, QKV_A_SUB = 512, A_QKV, 512
QKV_B_TM, QKV_B_TN, QKV_B_SUB = 1024, B_W, 512
PROJ_TM, PROJ_SUB = 512, 512
FFN_TM, FFN_TF = 1024, 512
PLE_TM, PLE_SUB = 512, 512
ATTN_A_QBLOCKS = 2
ATTN_B_TQ, ATTN_B_SUBS, ATTN_B_KCHUNK = 1024, 4, 2 * LANES

F32 = jnp.float32
BF16 = jnp.bfloat16


def _params(*sem, flags=None):
    return pltpu.CompilerParams(dimension_semantics=sem, vmem_limit_bytes=V7X_VMEM_LIMIT, flags=flags)


def _row_rms(x, gain):
    return x * lax.rsqrt(jnp.mean(x * x, axis=-1, keepdims=True) + EPS) * gain


def _dot(a, b):
    return jnp.dot(a, b, preferred_element_type=F32)


def _dot_nt(a, b):
    return lax.dot_general(a, b, (((1,), (1,)), ((), ())), preferred_element_type=F32)


def _qkv_kernel(x_ref, g_ref, w_ref, cg_ref, cm_ref, bd_ref, o_ref, hn_ref, *, group, n_norm, n_steps, sub):
    j = pl.program_id(1)
    tn = o_ref.shape[1]

    @pl.when(j == 0)
    def _():
        hn_ref[...] = _row_rms(x_ref[...], g_ref[...]).astype(BF16)

    def tiles(normed):
        for c in range(tn // sub):
            cols = slice(c * sub, (c + 1) * sub)
            o = _dot(hn_ref[...], w_ref[:, cols])
            if normed:
                sq = (o * o).astype(BF16)
                ss = jnp.concatenate(
                    [_dot(sq[:, t * MXU_DIM:(t + 1) * MXU_DIM], bd_ref[...]) for t in range(sub // MXU_DIM)],
                    axis=1)
                nrm = o * lax.rsqrt(ss * (1.0 / group) + EPS) * cg_ref[:, cols]
                o = jnp.where(cm_ref[:, cols] > 0.0, nrm, o)
            o_ref[:, cols] = o.astype(o_ref.dtype)

    pl.when(j < n_norm)(lambda: tiles(True))
    if n_norm < n_steps:
        pl.when(j >= n_norm)(lambda: tiles(False))


def _qkv_proj(x, gain, w_all, layer, col_gain, col_mask, group, tm, tn, sub, n_norm):
    m, d = x.shape
    n = w_all.shape[2]
    r = jnp.arange(MXU_DIM) // group
    bd = (r[:, None] == r[None, :]).astype(BF16)
    return pl.pallas_call(
        functools.partial(_qkv_kernel, group=group, n_norm=n_norm, n_steps=n // tn, sub=sub),
        grid=(m // tm, n // tn),
        in_specs=[
            pl.BlockSpec((tm, d), lambda i, j: (i, 0)),
            pl.BlockSpec((1, d), lambda i, j: (0, 0)),
            pl.BlockSpec((None, d, tn), lambda i, j: (layer, 0, j)),
            pl.BlockSpec((1, tn), lambda i, j: (0, j)),
            pl.BlockSpec((1, tn), lambda i, j: (0, j)),
            pl.BlockSpec((MXU_DIM, MXU_DIM), lambda i, j: (0, 0)),
        ],
        out_specs=pl.BlockSpec((tm, tn), lambda i, j: (i, j)),
        out_shape=jax.ShapeDtypeStruct((m, n), BF16),
        scratch_shapes=[pltpu.VMEM((tm, d), BF16)],
        compiler_params=_params("parallel", "arbitrary"),
        name="qkv_proj",
    )(x, gain.reshape(1, d), w_all, col_gain.reshape(1, n), col_mask.reshape(1, n), bd)


def _attn_a_kernel(slope_ref, sink_ref, hm_ref, q_ref, kvp_ref, kvc_ref, kvn_ref, o_ref, *, seq):
    kv_all = jnp.concatenate([kvp_ref[...], kvc_ref[...], kvn_ref[...]], axis=0)
    for u in range(ATTN_A_QBLOCKS):
        _attn_a_block(slope_ref, sink_ref, hm_ref, q_ref, o_ref, kv_all[u * BLK:u * BLK + BLK + 2 * WIN],
                      slice(u * BLK, (u + 1) * BLK), pl.program_id(1) * ATTN_A_QBLOCKS + u, seq)


def _attn_a_block(slope_ref, sink_ref, hm_ref, q_ref, o_ref, kv, rows, n, seq):
    span = BLK + 2 * WIN
    pairs = A_G // 2
    ti = lax.broadcasted_iota(jnp.int32, (BLK, span), 0)
    sj = lax.broadcasted_iota(jnp.int32, (BLK, span), 1)
    dist_i = jnp.abs(ti + WIN - sj)
    pos = sj + (n * BLK - WIN)
    valid = (dist_i <= WIN) & (pos >= 0) & (pos < seq)
    dist = jnp.where(valid, dist_i.astype(F32), MASKED_DIST)
    low_half = lax.broadcasted_iota(jnp.int32, (BLK, LANES), 1) < A_DH
    keep_lo = hm_ref[0:1, :]
    keep_hi = hm_ref[1:2, :]

    def block_diag(tile, head_in_low_half):
        swapped = jnp.concatenate([tile[:, A_DH:], tile[:, :A_DH]], axis=1)
        lo, hi = (tile, swapped) if head_in_low_half else (swapped, tile)
        return jnp.concatenate([lo * keep_lo, hi * keep_hi], axis=0)

    for kvh in range(A_KV):
        t0 = (kvh // 2) * LANES
        kbd = block_diag(kv[:, t0:t0 + LANES], kvh % 2 == 0)
        vbd = block_diag(kv[:, A_NK + t0:A_NK + t0 + LANES], kvh % 2 == 0)
        q0 = kvh * pairs * LANES
        qp = jnp.concatenate([q_ref[rows, q0 + pp * LANES:q0 + (pp + 1) * LANES] for pp in range(pairs)], axis=0)
        s = _dot_nt(qp, kbd)
        probs, scales = [], []
        for pp in range(pairs):
            halves, rls = [], []
            for e in range(2):
                h = kvh * A_G + 2 * pp + e
                sh = s[pp * BLK:(pp + 1) * BLK, e * span:(e + 1) * span] - slope_ref[h] * dist
                sink = sink_ref[h]
                mx = jnp.maximum(jnp.max(sh, axis=-1, keepdims=True), sink)
                p = jnp.exp2(sh - mx)
                l = jnp.sum(p, axis=-1, keepdims=True) + jnp.exp2(sink - mx)
                halves.append(p.astype(BF16))
                rls.append(1.0 / l)
            probs.append(jnp.concatenate(halves, axis=1))
            scales.append(jnp.where(low_half, rls[0], rls[1]))
        pv = _dot(jnp.concatenate(probs, axis=0), vbd)
        for pp in range(pairs):
            o_ref[rows, q0 + pp * LANES:q0 + (pp + 1) * LANES] = (
                pv[pp * BLK:(pp + 1) * BLK] * scales[pp]).astype(o_ref.dtype)


def _attn_a(qkv, slopes, sink, batch, seq):
    nb = seq // BLK
    qb = ATTN_A_QBLOCKS
    ns = nb // qb
    kv_col = A_NQ // (2 * A_NK)
    lane = jnp.arange(LANES)
    half_masks = jnp.stack([lane < A_DH, lane >= A_DH]).astype(BF16)
    return pl.pallas_call(
        functools.partial(_attn_a_kernel, seq=seq),
        grid=(batch, ns),
        in_specs=[
            pl.BlockSpec(memory_space=pltpu.SMEM),
            pl.BlockSpec(memory_space=pltpu.SMEM),
            pl.BlockSpec((2, LANES), lambda b, n: (0, 0)),
            pl.BlockSpec((qb * BLK, A_NQ), lambda b, n: (b * ns + n, 0)),
            pl.BlockSpec((BLK, 2 * A_NK), lambda b, n: (b * nb + jnp.maximum(n * qb - 1, 0), kv_col)),
            pl.BlockSpec((qb * BLK, 2 * A_NK), lambda b, n: (b * ns + n, kv_col)),
            pl.BlockSpec((BLK, 2 * A_NK), lambda b, n: (b * nb + jnp.minimum(n * qb + qb, nb - 1), kv_col)),
        ],
        out_specs=pl.BlockSpec((qb * BLK, A_NQ), lambda b, n: (b * ns + n, 0)),
        out_shape=jax.ShapeDtypeStruct((batch * seq, A_NQ), BF16),
        compiler_params=_params("parallel", "parallel"),
        name="attn_a",
    )(slopes, sink, half_masks, qkv, qkv, qkv, qkv)


def _attn_b_block(t, batch, nq):
    t = jnp.clip(t, 0, B_H * batch * nq - 1)
    return t // (nq * batch), (t // nq) % batch, t % nq


def _attn_b_kernel(slope_ref, lam_ref, sub_ref, q_ref, k_ref, v_ref, o_ref, bias_ref, s_ref, m_ref,
                   *, lambda_init, batch, nq):
    t = pl.program_id(0)
    tq = q_ref.shape[0]
    seq = k_ref.shape[0]
    ts = tq // ATTN_B_SUBS
    h, b, i = _attn_b_block(t, batch, nq)
    row0 = pl.multiple_of(i * tq, tq)

    @pl.when(t == 0)
    def _():
        s_ref[...] = jnp.zeros_like(s_ref)
        m_ref[...] = jnp.zeros_like(m_ref)

    @pl.when(b == 0)
    def _():
        tpos = (lax.broadcasted_iota(jnp.int32, (tq, seq), 0) + i * tq).astype(F32)
        spos = lax.broadcasted_iota(jnp.int32, (tq, seq), 1).astype(F32)
        bias_ref[pl.ds(row0, tq), :] = slope_ref[h] * jnp.abs(tpos - spos)

    lv = lam_ref[...]
    lam = (jnp.exp(jnp.sum(lv[0:1] * lv[1:2], axis=-1, keepdims=True))
           - jnp.exp(jnp.sum(lv[2:3] * lv[3:4], axis=-1, keepdims=True)) + lambda_init)

    for u in range(ATTN_B_SUBS):
        sub_rows = slice(u * ts, (u + 1) * ts)
        bias_rows = pl.ds(row0 + u * ts, ts)
        pv = [None, None]
        l_part = [None, None]
        m_run = [None, None]
        for n in range(seq // ATTN_B_KCHUNK):
            k0 = n * ATTN_B_KCHUNK
            keys = slice(k0, k0 + ATTN_B_KCHUNK)
            for c in range(2):
                m_rep = m_ref[u, c]
                p_lo = jnp.exp2(s_ref[u, c, :, k0:k0 + LANES] - m_rep)
                p_hi = jnp.exp2(s_ref[u, c, :, k0 + LANES:k0 + 2 * LANES] - m_rep)
                l_new = p_lo + p_hi
                l_part[c] = l_new if n == 0 else l_part[c] + l_new
                pv_new = _dot(jnp.concatenate([p_lo, p_hi], axis=1).astype(BF16), v_ref[keys, :])
                pv[c] = pv_new if n == 0 else pv[c] + pv_new
            for c in range(2):
                dims = slice(c * B_DH, (c + 1) * B_DH)
                s = _dot_nt(q_ref[sub_rows, dims], k_ref[keys, dims]) - bias_ref[bias_rows, keys]
                s_ref[u, c, :, keys] = s
                part = jnp.maximum(s[:, :LANES], s[:, LANES:])
                m_run[c] = part if n == 0 else jnp.maximum(m_run[c], part)
        l1 = jnp.sum(l_part[0], axis=-1, keepdims=True)
        l2 = jnp.sum(l_part[1], axis=-1, keepdims=True)
        o = pv[0] * (1.0 / l1) - pv[1] * (lam / l2)
        o_ref[sub_rows, :] = (_row_rms(o, sub_ref[...]) * (1.0 - lambda_init)).astype(o_ref.dtype)
        for c in range(2):
            m_ref[u, c] = jnp.broadcast_to(jnp.max(m_run[c], axis=-1, keepdims=True), (ts, LANES))


def _attn_b(qkv, slopes, lam_vecs, subln, batch, seq, lambda_init):
    tq = ATTN_B_TQ
    ts = tq // ATTN_B_SUBS
    nq = seq // tq
    e = 2 * B_DH

    def q_map(t):
        h, b, i = _attn_b_block(t, batch, nq)
        return b * nq + i, h

    def k_map(t):
        h, b, _ = _attn_b_block(t, batch, nq)
        return b, B_H + h

    def v_map(t):
        h, b, _ = _attn_b_block(t - 1, batch, nq)
        return b, 2 * B_H + h

    return pl.pallas_call(
        functools.partial(_attn_b_kernel, lambda_init=lambda_init, batch=batch, nq=nq),
        grid=(B_H * batch * nq + 1,),
        in_specs=[
            pl.BlockSpec(memory_space=pltpu.SMEM),
            pl.BlockSpec((4, B_DH), lambda t: (0, 0)),
            pl.BlockSpec((1, e), lambda t: (0, 0)),
            pl.BlockSpec((tq, e), q_map),
            pl.BlockSpec((seq, e), k_map),
            pl.BlockSpec((seq, e), v_map),
        ],
        out_specs=pl.BlockSpec((tq, e), lambda t: q_map(t - 1)),
        out_shape=jax.ShapeDtypeStruct((batch * seq, B_W), BF16),
        scratch_shapes=[pltpu.VMEM((seq, seq), F32),
                        pltpu.VMEM((ATTN_B_SUBS, 2, ts, seq), F32),
                        pltpu.VMEM((ATTN_B_SUBS, 2, ts, LANES), F32)],
        compiler_params=_params("arbitrary"),
        name="attn_b",
    )(slopes, lam_vecs, subln.reshape(1, e), qkv, qkv, qkv)


def _proj_res_kernel(a_ref, w_ref, r_ref, o_ref):
    for c in range(o_ref.shape[1] // PROJ_SUB):
        cols = slice(c * PROJ_SUB, (c + 1) * PROJ_SUB)
        o_ref[:, cols] = r_ref[:, cols] + _dot(a_ref[...], w_ref[:, cols])


def _proj_res(a, w_all, layer, res):
    m, kdim = a.shape
    n = w_all.shape[2]
    tm = PROJ_TM
    return pl.pallas_call(
        _proj_res_kernel,
        grid=(m // tm,),
        in_specs=[
            pl.BlockSpec((tm, kdim), lambda i: (i, 0)),
            pl.BlockSpec((None, kdim, n), lambda i: (layer, 0, 0)),
            pl.BlockSpec((tm, n), lambda i: (i, 0)),
        ],
        out_specs=pl.BlockSpec((tm, n), lambda i: (i, 0)),
        out_shape=jax.ShapeDtypeStruct((m, n), F32),
        compiler_params=_params("parallel"),
        name="proj_res",
    )(a, w_all, res)


def _ffn_kernel(x_ref, g_ref, wg_ref, wu_ref, wo_ref, o_ref, hn_ref):
    f = pl.program_id(1)

    @pl.when(f == 0)
    def _():
        x = x_ref[...]
        hn_ref[...] = _row_rms(x, g_ref[...]).astype(BF16)
        o_ref[...] = x

    hn = hn_ref[...]
    gate = _dot(hn, wg_ref[...])
    up = _dot(hn, wu_ref[...])
    act = (gate * jax.nn.sigmoid(gate) * up).astype(BF16)
    o_ref[...] += _dot(act, wo_ref[...])


def _ffn(x, gain, w_in_all, w_out_all, layer):
    m, d = x.shape
    tm, tf = FFN_TM, FFN_TF
    nf = D_FF // tf
    return pl.pallas_call(
        _ffn_kernel,
        grid=(m // tm, nf),
        in_specs=[
            pl.BlockSpec((tm, d), lambda i, f: (i, 0)),
            pl.BlockSpec((1, d), lambda i, f: (0, 0)),
            pl.BlockSpec((None, d, tf), lambda i, f: (layer, 0, f)),
            pl.BlockSpec((None, d, tf), lambda i, f: (layer, 0, nf + f)),
            pl.BlockSpec((None, tf, d), lambda i, f: (layer, f, 0)),
        ],
        out_specs=pl.BlockSpec((tm, d), lambda i, f: (i, 0)),
        out_shape=jax.ShapeDtypeStruct((m, d), F32),
        scratch_shapes=[pltpu.VMEM((tm, d), BF16)],
        compiler_params=_params("parallel", "arbitrary"),
        name="ffn",
    )(x, gain.reshape(1, d), w_in_all, w_in_all, w_out_all)


def _ple_kernel(h_ref, p_ref, gn_ref, wg_ref, wp_ref, pn_ref, o_ref):
    hn = _row_rms(h_ref[...], gn_ref[...]).astype(BF16)
    pe = _row_rms(_dot(p_ref[...].astype(BF16), wp_ref[...]), pn_ref[...])
    for c in range(o_ref.shape[1] // PLE_SUB):
        cols = slice(c * PLE_SUB, (c + 1) * PLE_SUB)
        gate = jax.nn.sigmoid(_dot(hn, wg_ref[:, cols]))
        o_ref[:, cols] = h_ref[:, cols] + pe[:, cols] * gate


def _ple(h, p_all, gate_norm, w_gate_all, w_proj_all, post_norm, layer):
    m, d = h.shape
    tm = PLE_TM
    return pl.pallas_call(
        _ple_kernel,
        grid=(m // tm,),
        in_specs=[
            pl.BlockSpec((tm, d), lambda i: (i, 0)),
            pl.BlockSpec((None, tm, PLE_DIM), lambda i: (layer, i, 0)),
            pl.BlockSpec((1, d), lambda i: (0, 0)),
            pl.BlockSpec((None, d, d), lambda i: (layer, 0, 0)),
            pl.BlockSpec((None, PLE_DIM, d), lambda i: (layer, 0, 0)),
            pl.BlockSpec((1, d), lambda i: (0, 0)),
        ],
        out_specs=pl.BlockSpec((tm, d), lambda i: (i, 0)),
        out_shape=jax.ShapeDtypeStruct((m, d), F32),
        compiler_params=_params("parallel"),
        name="ple",
    )(h, p_all, gate_norm.reshape(1, d), w_gate_all, w_proj_all, post_norm.reshape(1, d))


def _alibi_slopes(n):
    return jnp.exp2(-8.0 * jnp.arange(1, n + 1, dtype=F32) / n)


def kernel(x, p, attn_norm, ffn_norm, a_w_qkv, a_q_norm, a_k_norm, a_sink, a_w_o, b_w_qkv, b_q_norm, b_k_norm, b_lambda, b_subln, b_w_o, w_ffn_in, w_ffn_out, ple_w_proj, ple_post_norm, ple_gate_norm, ple_w_gate):
    batch, seq, d = x.shape
    m = batch * seq
    h = x.reshape(m, d)
    p_all = p.reshape(DEPTH, m, PLE_DIM)
    a_w_qkv, a_w_o, b_w_qkv, b_w_o, w_ffn_in, w_ffn_out, ple_w_proj, ple_w_gate = (
        w.astype(BF16) for w in (a_w_qkv, a_w_o, b_w_qkv, b_w_o, w_ffn_in, w_ffn_out, ple_w_proj, ple_w_gate))
    for i in range(DEPTH):
        j = i // 2
        if i % 2 == 0:
            col_gain = jnp.concatenate([jnp.tile(a_q_norm[j] * (A_DH ** -0.5 * LOG2E), A_HQ),
                                        jnp.tile(a_k_norm[j], A_KV), jnp.ones((A_NK,), F32)])
            col_mask = jnp.concatenate([jnp.ones((A_NQ + A_NK,), F32), jnp.zeros((A_NK,), F32)])
            qkv = _qkv_proj(h, attn_norm[i], a_w_qkv, j, col_gain, col_mask, A_DH,
                            QKV_A_TM, QKV_A_TN, QKV_A_SUB, A_QKV // QKV_A_TN)
            mix = _attn_a(qkv, _alibi_slopes(A_HQ) * LOG2E, a_sink[j].astype(F32) * LOG2E, batch, seq)
            h = _proj_res(mix, a_w_o, j, h)
        else:
            lambda_init = 0.8 - 0.6 * math.exp(-0.3 * i)
            col_gain = jnp.concatenate([jnp.tile(b_q_norm[j] * (B_DH ** -0.5 * LOG2E), 2 * B_H),
                                        jnp.tile(b_k_norm[j], 2 * B_H), jnp.ones((B_W,), F32)])
            col_mask = jnp.concatenate([jnp.ones((2 * B_W,), F32), jnp.zeros((B_W,), F32)])
            qkv = _qkv_proj(h, attn_norm[i], b_w_qkv, j, col_gain, col_mask, B_DH,
                            QKV_B_TM, QKV_B_TN, QKV_B_SUB, 2 * B_W // QKV_B_TN)
            mix = _attn_b(qkv, _alibi_slopes(B_H) * LOG2E, b_lambda[j], b_subln[j], batch, seq, lambda_init)
            h = _proj_res(mix, b_w_o, j, h)
        h = _ffn(h, ffn_norm[i], w_ffn_in, w_ffn_out, i)
        h = _ple(h, p_all, ple_gate_norm[i], ple_w_gate, ple_w_proj, ple_post_norm[i], i)
    return h.reshape(batch, seq, d)
```

```python
import functools
import math

import jax
import jax.numpy as jnp
from jax import lax
from jax.experimental import pallas as pl
from jax.experimental.pallas import tpu as pltpu

D_MODEL = 2048
DEPTH = 2
BLK = 128
WIN = 128
A_DH = 64
A_HQ = D_MODEL // A_DH
A_KV = 4
A_G = A_HQ // A_KV
A_NQ = A_HQ * A_DH
A_NK = A_KV * A_DH
A_QKV = A_NQ + 2 * A_NK
B_DH = 128
B_H = D_MODEL // (2 * B_DH)
B_W = B_H * 2 * B_DH
B_QKV = 3 * B_W
D_FF = int(math.ceil(8 * D_MODEL / 3 / 256) * 256)
PLE_DIM = 256
EPS = 1e-6
MASKED_DIST = 1e30
LOG2E = math.log2(math.e)

V7X_VMEM_LIMIT = 60 * 1024 * 1024
LANES = 128
MXU_DIM = 256
BF16_TILE_ROWS = 16
QKV_A_TM, QKV_A_TN, QKV_A_SUB = 512, A_QKV, 512
QKV_B_TM, QKV_B_TN, QKV_B_SUB = 1024, B_W, 512
PROJ_TM, PROJ_SUB = 512, 512
FFN_TM, FFN_TF = 1024, 512
PLE_TM, PLE_SUB = 512, 512
ATTN_A_QBLOCKS = 2
ATTN_B_TQ, ATTN_B_SUBS, ATTN_B_KCHUNK = 1024, 4, 2 * LANES

F32 = jnp.float32
BF16 = jnp.bfloat16


def _params(*sem):
    return pltpu.CompilerParams(dimension_semantics=sem, vmem_limit_bytes=V7X_VMEM_LIMIT)


def _row_rms(x, gain):
    return x * lax.rsqrt(jnp.mean(x * x, axis=-1, keepdims=True) + EPS) * gain


def _dot(a, b):
    return jnp.dot(a, b, preferred_element_type=F32)


def _dot_nt(a, b):
    return lax.dot_general(a, b, (((1,), (1,)), ((), ())), preferred_element_type=F32)


def _cast_plan(weights, n_steps):
    plan = []
    for w_all, layer in weights:
        tiles = w_all.shape[1] // BF16_TILE_ROWS
        nblk = max(k for k in range(1, min(tiles, n_steps) + 1) if tiles % k == 0)
        plan.append((w_all, layer, nblk, w_all.shape[1] // nblk))
    return plan


def _cast_specs(plan, step_of):
    in_specs, out_specs, out_shapes = [], [], []
    for w_all, layer, nblk, rows in plan:
        cols = w_all.shape[2]
        in_specs.append(pl.BlockSpec(
            (None, rows, cols),
            lambda *ids, layer=layer, nblk=nblk: (layer, jnp.minimum(step_of(*ids), nblk - 1), 0)))
        out_specs.append(pl.BlockSpec(
            (rows, cols), lambda *ids, nblk=nblk: (jnp.minimum(step_of(*ids), nblk - 1), 0)))
        out_shapes.append(jax.ShapeDtypeStruct(w_all.shape[1:], BF16))
    return in_specs, out_specs, out_shapes


def _cast_blocks(w_refs, o_refs):
    for w_ref, o_ref in zip(w_refs, o_refs):
        o_ref[...] = w_ref[...].astype(BF16)


def _qkv_kernel(x_ref, g_ref, w_ref, cg_ref, cm_ref, bd_ref, o_ref, hn_ref, *, group, n_norm, n_steps, sub):
    j = pl.program_id(1)
    tn = o_ref.shape[1]

    @pl.when(j == 0)
    def _():
        hn_ref[...] = _row_rms(x_ref[...], g_ref[...]).astype(BF16)

    def tiles(normed):
        for c in range(tn // sub):
            cols = slice(c * sub, (c + 1) * sub)
            o = _dot(hn_ref[...], w_ref[:, cols])
            if normed:
                sq = (o * o).astype(BF16)
                ss = jnp.concatenate(
                    [_dot(sq[:, t * MXU_DIM:(t + 1) * MXU_DIM], bd_ref[...]) for t in range(sub // MXU_DIM)],
                    axis=1)
                nrm = o * lax.rsqrt(ss * (1.0 / group) + EPS) * cg_ref[:, cols]
                o = jnp.where(cm_ref[:, cols] > 0.0, nrm, o)
            o_ref[:, cols] = o.astype(o_ref.dtype)

    pl.when(j < n_norm)(lambda: tiles(True))
    if n_norm < n_steps:
        pl.when(j >= n_norm)(lambda: tiles(False))


def _qkv_proj(x, gain, w, col_gain, col_mask, group, tm, tn, sub, n_norm):
    m, d = x.shape
    n = w.shape[1]
    r = jnp.arange(MXU_DIM) // group
    bd = (r[:, None] == r[None, :]).astype(BF16)
    return pl.pallas_call(
        functools.partial(_qkv_kernel, group=group, n_norm=n_norm, n_steps=n // tn, sub=sub),
        grid=(m // tm, n // tn),
        in_specs=[
            pl.BlockSpec((tm, d), lambda i, j: (i, 0)),
            pl.BlockSpec((1, d), lambda i, j: (0, 0)),
            pl.BlockSpec((d, tn), lambda i, j: (0, j)),
            pl.BlockSpec((1, tn), lambda i, j: (0, j)),
            pl.BlockSpec((1, tn), lambda i, j: (0, j)),
            pl.BlockSpec((MXU_DIM, MXU_DIM), lambda i, j: (0, 0)),
        ],
        out_specs=pl.BlockSpec((tm, tn), lambda i, j: (i, j)),
        out_shape=jax.ShapeDtypeStruct((m, n), BF16),
        scratch_shapes=[pltpu.VMEM((tm, d), BF16)],
        compiler_params=_params("parallel", "arbitrary"),
        name="qkv_proj",
    )(x, gain.reshape(1, d), w, col_gain.reshape(1, n), col_mask.reshape(1, n), bd)


def _attn_a_kernel(slope_ref, sink_ref, hm_ref, q_ref, kvp_ref, kvc_ref, kvn_ref, *rest, seq, n_cast):
    w_refs, o_ref, wo_refs = rest[:n_cast], rest[n_cast], rest[n_cast + 1:]
    _cast_blocks(w_refs, wo_refs)
    kv_all = jnp.concatenate([kvp_ref[...], kvc_ref[...], kvn_ref[...]], axis=0)
    for u in range(ATTN_A_QBLOCKS):
        _attn_a_block(slope_ref, sink_ref, hm_ref, q_ref, o_ref, kv_all[u * BLK:u * BLK + BLK + 2 * WIN],
                      slice(u * BLK, (u + 1) * BLK), pl.program_id(1) * ATTN_A_QBLOCKS + u, seq)


def _attn_a_block(slope_ref, sink_ref, hm_ref, q_ref, o_ref, kv, rows, n, seq):
    span = BLK + 2 * WIN
    pairs = A_G // 2
    ti = lax.broadcasted_iota(jnp.int32, (BLK, span), 0)
    sj = lax.broadcasted_iota(jnp.int32, (BLK, span), 1)
    dist_i = jnp.abs(ti + WIN - sj)
    pos = sj + (n * BLK - WIN)
    valid = (dist_i <= WIN) & (pos >= 0) & (pos < seq)
    dist = jnp.where(valid, dist_i.astype(F32), MASKED_DIST)
    low_half = lax.broadcasted_iota(jnp.int32, (BLK, LANES), 1) < A_DH
    keep_lo = hm_ref[0:1, :]
    keep_hi = hm_ref[1:2, :]

    def block_diag(tile, head_in_low_half):
        swapped = jnp.concatenate([tile[:, A_DH:], tile[:, :A_DH]], axis=1)
        lo, hi = (tile, swapped) if head_in_low_half else (swapped, tile)
        return jnp.concatenate([lo * keep_lo, hi * keep_hi], axis=0)

    for kvh in range(A_KV):
        t0 = (kvh // 2) * LANES
        kbd = block_diag(kv[:, t0:t0 + LANES], kvh % 2 == 0)
        vbd = block_diag(kv[:, A_NK + t0:A_NK + t0 + LANES], kvh % 2 == 0)
        q0 = kvh * pairs * LANES
        qp = jnp.concatenate([q_ref[rows, q0 + pp * LANES:q0 + (pp + 1) * LANES] for pp in range(pairs)], axis=0)
        s = _dot_nt(qp, kbd)
        probs, scales = [], []
        for pp in range(pairs):
            halves, rls = [], []
            for e in range(2):
                h = kvh * A_G + 2 * pp + e
                sh = s[pp * BLK:(pp + 1) * BLK, e * span:(e + 1) * span] - slope_ref[h] * dist
                sink = sink_ref[h]
                mx = jnp.maximum(jnp.max(sh, axis=-1, keepdims=True), sink)
                p = jnp.exp2(sh - mx)
                l = jnp.sum(p, axis=-1, keepdims=True) + jnp.exp2(sink - mx)
                halves.append(p.astype(BF16))
                rls.append(1.0 / l)
            probs.append(jnp.concatenate(halves, axis=1))
            scales.append(jnp.where(low_half, rls[0], rls[1]))
        pv = _dot(jnp.concatenate(probs, axis=0), vbd)
        for pp in range(pairs):
            o_ref[rows, q0 + pp * LANES:q0 + (pp + 1) * LANES] = (
                pv[pp * BLK:(pp + 1) * BLK] * scales[pp]).astype(o_ref.dtype)


def _attn_a(qkv, slopes, sink, batch, seq, cast_weights):
    nb = seq // BLK
    qb = ATTN_A_QBLOCKS
    ns = nb // qb
    kv_col = A_NQ // (2 * A_NK)
    lane = jnp.arange(LANES)
    half_masks = jnp.stack([lane < A_DH, lane >= A_DH]).astype(BF16)
    plan = _cast_plan(cast_weights, batch * ns)
    cast_in, cast_out, cast_shapes = _cast_specs(plan, lambda b, n: b * ns + n)
    outs = pl.pallas_call(
        functools.partial(_attn_a_kernel, seq=seq, n_cast=len(plan)),
        grid=(batch, ns),
        in_specs=[
            pl.BlockSpec(memory_space=pltpu.SMEM),
            pl.BlockSpec(memory_space=pltpu.SMEM),
            pl.BlockSpec((2, LANES), lambda b, n: (0, 0)),
            pl.BlockSpec((qb * BLK, A_NQ), lambda b, n: (b * ns + n, 0)),
            pl.BlockSpec((BLK, 2 * A_NK), lambda b, n: (b * nb + jnp.maximum(n * qb - 1, 0), kv_col)),
            pl.BlockSpec((qb * BLK, 2 * A_NK), lambda b, n: (b * ns + n, kv_col)),
            pl.BlockSpec((BLK, 2 * A_NK), lambda b, n: (b * nb + jnp.minimum(n * qb + qb, nb - 1), kv_col)),
        ] + cast_in,
        out_specs=[pl.BlockSpec((qb * BLK, A_NQ), lambda b, n: (b * ns + n, 0))] + cast_out,
        out_shape=[jax.ShapeDtypeStruct((batch * seq, A_NQ), BF16)] + cast_shapes,
        compiler_params=_params("arbitrary", "arbitrary"),
        name="attn_a",
    )(slopes, sink, half_masks, qkv, qkv, qkv, qkv, *[p[0] for p in plan])
    return outs[0], outs[1:]


def _attn_b_block(t, batch, nq):
    t = jnp.clip(t, 0, B_H * batch * nq - 1)
    return t // (nq * batch), (t // nq) % batch, t % nq


def _attn_b_kernel(slope_ref, lam_ref, sub_ref, q_ref, k_ref, v_ref, *rest, lambda_init, batch, nq, n_cast):
    w_refs, o_ref, wo_refs = rest[:n_cast], rest[n_cast], rest[n_cast + 1:2 * n_cast + 1]
    bias_ref, s_ref, m_ref = rest[2 * n_cast + 1:]
    t = pl.program_id(0)
    tq = q_ref.shape[0]
    seq = k_ref.shape[0]
    ts = tq // ATTN_B_SUBS
    h, b, i = _attn_b_block(t, batch, nq)
    row0 = pl.multiple_of(i * tq, tq)

    @pl.when(t == 0)
    def _():
        s_ref[...] = jnp.zeros_like(s_ref)
        m_ref[...] = jnp.zeros_like(m_ref)

    @pl.when(b == 0)
    def _():
        tpos = (lax.broadcasted_iota(jnp.int32, (tq, seq), 0) + i * tq).astype(F32)
        spos = lax.broadcasted_iota(jnp.int32, (tq, seq), 1).astype(F32)
        bias_ref[pl.ds(row0, tq), :] = slope_ref[h] * jnp.abs(tpos - spos)

    lv = lam_ref[...]
    lam = (jnp.exp(jnp.sum(lv[0:1] * lv[1:2], axis=-1, keepdims=True))
           - jnp.exp(jnp.sum(lv[2:3] * lv[3:4], axis=-1, keepdims=True)) + lambda_init)

    _cast_blocks(w_refs, wo_refs)
    for u in range(ATTN_B_SUBS):
        sub_rows = slice(u * ts, (u + 1) * ts)
        bias_rows = pl.ds(row0 + u * ts, ts)
        pv = [None, None]
        l_part = [None, None]
        m_run = [None, None]
        for n in range(seq // ATTN_B_KCHUNK):
            k0 = n * ATTN_B_KCHUNK
            keys = slice(k0, k0 + ATTN_B_KCHUNK)
            for c in range(2):
                m_rep = m_ref[u, c]
                p_lo = jnp.exp2(s_ref[u, c, :, k0:k0 + LANES] - m_rep)
                p_hi = jnp.exp2(s_ref[u, c, :, k0 + LANES:k0 + 2 * LANES] - m_rep)
                l_new = p_lo + p_hi
                l_part[c] = l_new if n == 0 else l_part[c] + l_new
                pv_new = _dot(jnp.concatenate([p_lo, p_hi], axis=1).astype(BF16), v_ref[keys, :])
                pv[c] = pv_new if n == 0 else pv[c] + pv_new
            for c in range(2):
                dims = slice(c * B_DH, (c + 1) * B_DH)
                s = _dot_nt(q_ref[sub_rows, dims], k_ref[keys, dims]) - bias_ref[bias_rows, keys]
                s_ref[u, c, :, keys] = s
                part = jnp.maximum(s[:, :LANES], s[:, LANES:])
                m_run[c] = part if n == 0 else jnp.maximum(m_run[c], part)
        l1 = jnp.sum(l_part[0], axis=-1, keepdims=True)
        l2 = jnp.sum(l_part[1], axis=-1, keepdims=True)
        o = pv[0] * (1.0 / l1) - pv[1] * (lam / l2)
        o_ref[sub_rows, :] = (_row_rms(o, sub_ref[...]) * (1.0 - lambda_init)).astype(o_ref.dtype)
        for c in range(2):
            m_ref[u, c] = jnp.broadcast_to(jnp.max(m_run[c], axis=-1, keepdims=True), (ts, LANES))


def _attn_b(qkv, slopes, lam_vecs, subln, batch, seq, lambda_init, cast_weights):
    tq = ATTN_B_TQ
    ts = tq // ATTN_B_SUBS
    nq = seq // tq
    e = 2 * B_DH
    n_steps = B_H * batch * nq + 1

    def q_map(t):
        h, b, i = _attn_b_block(t, batch, nq)
        return b * nq + i, h

    def k_map(t):
        h, b, _ = _attn_b_block(t, batch, nq)
        return b, B_H + h

    def v_map(t):
        h, b, _ = _attn_b_block(t - 1, batch, nq)
        return b, 2 * B_H + h

    plan = _cast_plan(cast_weights, n_steps)
    cast_in, cast_out, cast_shapes = _cast_specs(plan, lambda t: t)
    outs = pl.pallas_call(
        functools.partial(_attn_b_kernel, lambda_init=lambda_init, batch=batch, nq=nq, n_cast=len(plan)),
        grid=(n_steps,),
        in_specs=[
            pl.BlockSpec(memory_space=pltpu.SMEM),
            pl.BlockSpec((4, B_DH), lambda t: (0, 0)),
            pl.BlockSpec((1, e), lambda t: (0, 0)),
            pl.BlockSpec((tq, e), q_map),
            pl.BlockSpec((seq, e), k_map),
            pl.BlockSpec((seq, e), v_map),
        ] + cast_in,
        out_specs=[pl.BlockSpec((tq, e), lambda t: q_map(t - 1))] + cast_out,
        out_shape=[jax.ShapeDtypeStruct((batch * seq, B_W), BF16)] + cast_shapes,
        scratch_shapes=[pltpu.VMEM((seq, seq), F32),
                        pltpu.VMEM((ATTN_B_SUBS, 2, ts, seq), F32),
                        pltpu.VMEM((ATTN_B_SUBS, 2, ts, LANES), F32)],
        compiler_params=_params("arbitrary"),
        name="attn_b",
    )(slopes, lam_vecs, subln.reshape(1, e), qkv, qkv, qkv, *[p[0] for p in plan])
    return outs[0], outs[1:]


def _proj_res_kernel(a_ref, w_ref, r_ref, o_ref):
    for c in range(o_ref.shape[1] // PROJ_SUB):
        cols = slice(c * PROJ_SUB, (c + 1) * PROJ_SUB)
        o_ref[:, cols] = r_ref[:, cols] + _dot(a_ref[...], w_ref[:, cols])


def _proj_res(a, w, res):
    m, kdim = a.shape
    n = w.shape[1]
    tm = PROJ_TM
    return pl.pallas_call(
        _proj_res_kernel,
        grid=(m // tm,),
        in_specs=[
            pl.BlockSpec((tm, kdim), lambda i: (i, 0)),
            pl.BlockSpec((kdim, n), lambda i: (0, 0)),
            pl.BlockSpec((tm, n), lambda i: (i, 0)),
        ],
        out_specs=pl.BlockSpec((tm, n), lambda i: (i, 0)),
        out_shape=jax.ShapeDtypeStruct((m, n), F32),
        compiler_params=_params("parallel"),
        name="proj_res",
    )(a, w, res)


def _ffn_kernel(x_ref, g_ref, wg_ref, wu_ref, wo_ref, o_ref, hn_ref):
    f = pl.program_id(1)

    @pl.when(f == 0)
    def _():
        x = x_ref[...]
        hn_ref[...] = _row_rms(x, g_ref[...]).astype(BF16)
        o_ref[...] = x

    hn = hn_ref[...]
    gate = _dot(hn, wg_ref[...])
    up = _dot(hn, wu_ref[...])
    act = (gate * jax.nn.sigmoid(gate) * up).astype(BF16)
    o_ref[...] += _dot(act, wo_ref[...])


def _ffn(x, gain, w_in, w_out):
    m, d = x.shape
    tm, tf = FFN_TM, FFN_TF
    nf = D_FF // tf
    return pl.pallas_call(
        _ffn_kernel,
        grid=(m // tm, nf),
        in_specs=[
            pl.BlockSpec((tm, d), lambda i, f: (i, 0)),
            pl.BlockSpec((1, d), lambda i, f: (0, 0)),
            pl.BlockSpec((d, tf), lambda i, f: (0, f)),
            pl.BlockSpec((d, tf), lambda i, f: (0, nf + f)),
            pl.BlockSpec((tf, d), lambda i, f: (f, 0)),
        ],
        out_specs=pl.BlockSpec((tm, d), lambda i, f: (i, 0)),
        out_shape=jax.ShapeDtypeStruct((m, d), F32),
        scratch_shapes=[pltpu.VMEM((tm, d), BF16)],
        compiler_params=_params("parallel", "arbitrary"),
        name="ffn",
    )(x, gain.reshape(1, d), w_in, w_in, w_out)


def _ple_kernel(h_ref, p_ref, gn_ref, wg_ref, wp_ref, pn_ref, o_ref):
    hn = _row_rms(h_ref[...], gn_ref[...]).astype(BF16)
    pe = _row_rms(_dot(p_ref[...].astype(BF16), wp_ref[...]), pn_ref[...])
    for c in range(o_ref.shape[1] // PLE_SUB):
        cols = slice(c * PLE_SUB, (c + 1) * PLE_SUB)
        gate = jax.nn.sigmoid(_dot(hn, wg_ref[:, cols]))
        o_ref[:, cols] = h_ref[:, cols] + pe[:, cols] * gate


def _ple(h, p_all, layer, gate_norm, w_gate, w_proj, post_norm):
    m, d = h.shape
    tm = PLE_TM
    return pl.pallas_call(
        _ple_kernel,
        grid=(m // tm,),
        in_specs=[
            pl.BlockSpec((tm, d), lambda i: (i, 0)),
            pl.BlockSpec((None, tm, PLE_DIM), lambda i: (layer, i, 0)),
            pl.BlockSpec((1, d), lambda i: (0, 0)),
            pl.BlockSpec((d, d), lambda i: (0, 0)),
            pl.BlockSpec((PLE_DIM, d), lambda i: (0, 0)),
            pl.BlockSpec((1, d), lambda i: (0, 0)),
        ],
        out_specs=pl.BlockSpec((tm, d), lambda i: (i, 0)),
        out_shape=jax.ShapeDtypeStruct((m, d), F32),
        compiler_params=_params("parallel"),
        name="ple",
    )(h, p_all, gate_norm.reshape(1, d), w_gate, w_proj, post_norm.reshape(1, d))


def _alibi_slopes(n):
    return jnp.exp2(-8.0 * jnp.arange(1, n + 1, dtype=F32) / n)


def kernel(x, p, attn_norm, ffn_norm, a_w_qkv, a_q_norm, a_k_norm, a_sink, a_w_o, b_w_qkv, b_q_norm, b_k_norm, b_lambda, b_subln, b_w_o, w_ffn_in, w_ffn_out, ple_w_proj, ple_post_norm, ple_gate_norm, ple_w_gate):
    batch, seq, d = x.shape
    m = batch * seq
    h = x.reshape(m, d)
    p_all = p.reshape(DEPTH, m, PLE_DIM)
    w_qkv = a_w_qkv[0].astype(BF16)
    for i in range(DEPTH):
        j = i // 2
        if i % 2 == 0:
            col_gain = jnp.concatenate([jnp.tile(a_q_norm[j] * (A_DH ** -0.5 * LOG2E), A_HQ),
                                        jnp.tile(a_k_norm[j], A_KV), jnp.ones((A_NK,), F32)])
            col_mask = jnp.concatenate([jnp.ones((A_NQ + A_NK,), F32), jnp.zeros((A_NK,), F32)])
            qkv = _qkv_proj(h, attn_norm[i], w_qkv, col_gain, col_mask, A_DH,
                            QKV_A_TM, QKV_A_TN, QKV_A_SUB, A_QKV // QKV_A_TN)
            mix, (w_o, w_in, w_out, w_gate, w_proj, w_qkv) = _attn_a(
                qkv, _alibi_slopes(A_HQ) * LOG2E, a_sink[j].astype(F32) * LOG2E, batch, seq,
                [(a_w_o, j), (w_ffn_in, i), (w_ffn_out, i), (ple_w_gate, i), (ple_w_proj, i), (b_w_qkv, j)])
        else:
            lambda_init = 0.8 - 0.6 * math.exp(-0.3 * i)
            col_gain = jnp.concatenate([jnp.tile(b_q_norm[j] * (B_DH ** -0.5 * LOG2E), 2 * B_H),
                                        jnp.tile(b_k_norm[j], 2 * B_H), jnp.ones((B_W,), F32)])
            col_mask = jnp.concatenate([jnp.ones((2 * B_W,), F32), jnp.zeros((B_W,), F32)])
            qkv = _qkv_proj(h, attn_norm[i], w_qkv, col_gain, col_mask, B_DH,
                            QKV_B_TM, QKV_B_TN, QKV_B_SUB, 2 * B_W // QKV_B_TN)
            mix, (w_o, w_in, w_out, w_gate, w_proj) = _attn_b(
                qkv, _alibi_slopes(B_H) * LOG2E, b_lambda[j], b_subln[j], batch, seq, lambda_init,
                [(b_w_o, j), (w_ffn_in, i), (w_ffn_out, i), (ple_w_gate, i), (ple_w_proj, i)])
        h = _proj_res(mix, w_o, h)
        h = _ffn(h, ffn_norm[i], w_in, w_out)
        h = _ple(h, p_all, i, ple_gate_norm[i], w_gate, w_proj, ple_post_norm[i])
    return h.reshape(batch, seq, d)
```

```python
import functools
import math

import jax
import jax.numpy as jnp
from jax import lax
from jax.experimental import pallas as pl
from jax.experimental.pallas import tpu as pltpu

D_MODEL = 2048
DEPTH = 2
BLK = 128
WIN = 128
A_DH = 64
A_HQ = D_MODEL // A_DH
A_KV = 4
A_G = A_HQ // A_KV
A_NQ = A_HQ * A_DH
A_NK = A_KV * A_DH
A_QKV = A_NQ + 2 * A_NK
B_DH = 128
B_H = D_MODEL // (2 * B_DH)
B_W = B_H * 2 * B_DH
B_QKV = 3 * B_W
D_FF = int(math.ceil(8 * D_MODEL / 3 / 256) * 256)
PLE_DIM = 256
EPS = 1e-6
MASKED_DIST = 1e30
LOG2E = math.log2(math.e)

V7X_VMEM_LIMIT = 60 * 1024 * 1024
LANES = 128
MXU_DIM = 256
BF16_TILE_ROWS = 16
QKV_A_TM, QKV_A_TN, QKV_A_SUB = 512, A_QKV, 512
QKV_B_TM, QKV_B_TN, QKV_B_SUB = 1024, B_W, 512
PROJ_TM, PROJ_SUB = 512, 512
FFN_TM, FFN_TF = 1024, 512
PLE_TM, PLE_SUB = 512, 512
ATTN_A_QBLOCKS = 2
ATTN_B_TQ, ATTN_B_SUBS, ATTN_B_KCHUNK = 1024, 4, 2 * LANES

F32 = jnp.float32
BF16 = jnp.bfloat16


def _params(*sem):
    return pltpu.CompilerParams(dimension_semantics=sem, vmem_limit_bytes=V7X_VMEM_LIMIT)


def _row_rms(x, gain):
    return x * lax.rsqrt(jnp.mean(x * x, axis=-1, keepdims=True) + EPS) * gain


def _dot(a, b):
    return jnp.dot(a, b, preferred_element_type=F32)


def _dot_nt(a, b):
    return lax.dot_general(a, b, (((1,), (1,)), ((), ())), preferred_element_type=F32)


def _cast_plan(weights, n_steps):
    plan = []
    for w_all, layer in weights:
        tiles = w_all.shape[1] // BF16_TILE_ROWS
        nblk = max(k for k in range(1, min(tiles, n_steps) + 1) if tiles % k == 0)
        plan.append((w_all, layer, nblk, w_all.shape[1] // nblk))
    return plan


def _cast_specs(plan, step_of):
    in_specs, out_specs, out_shapes = [], [], []
    for w_all, layer, nblk, rows in plan:
        cols = w_all.shape[2]
        in_specs.append(pl.BlockSpec(
            (None, rows, cols),
            lambda *ids, layer=layer, nblk=nblk: (layer, jnp.minimum(step_of(*ids), nblk - 1), 0)))
        out_specs.append(pl.BlockSpec(
            (rows, cols), lambda *ids, nblk=nblk: (jnp.minimum(step_of(*ids), nblk - 1), 0)))
        out_shapes.append(jax.ShapeDtypeStruct(w_all.shape[1:], BF16))
    return in_specs, out_specs, out_shapes


def _cast_blocks(w_refs, o_refs):
    for w_ref, o_ref in zip(w_refs, o_refs):
        o_ref[...] = w_ref[...].astype(BF16)


def _head_sumsq(tile, group):
    sq = tile * tile
    total = jnp.sum(sq, axis=-1, keepdims=True)
    if group == LANES:
        return total
    assert 2 * group == LANES
    low = lax.broadcasted_iota(jnp.int32, tile.shape, 1) < group
    low_sum = jnp.sum(jnp.where(low, sq, 0.0), axis=-1, keepdims=True)
    return jnp.where(low, low_sum, total - low_sum)


def _qkv_kernel(x_ref, g_ref, w_ref, cg_ref, cm_ref, o_ref, hn_ref, *, group, n_norm, n_steps, sub):
    j = pl.program_id(1)
    tn = o_ref.shape[1]

    @pl.when(j == 0)
    def _():
        hn_ref[...] = _row_rms(x_ref[...], g_ref[...]).astype(BF16)

    def tiles(normed):
        for c in range(tn // sub):
            cols = slice(c * sub, (c + 1) * sub)
            o = _dot(hn_ref[...], w_ref[:, cols])
            if normed:
                parts = []
                for t in range(sub // LANES):
                    tile = o[:, t * LANES:(t + 1) * LANES]
                    parts.append(tile * lax.rsqrt(_head_sumsq(tile, group) * (1.0 / group) + EPS))
                nrm = jnp.concatenate(parts, axis=1) * cg_ref[:, cols]
                o = jnp.where(cm_ref[:, cols] > 0.0, nrm, o)
            o_ref[:, cols] = o.astype(o_ref.dtype)

    pl.when(j < n_norm)(lambda: tiles(True))
    if n_norm < n_steps:
        pl.when(j >= n_norm)(lambda: tiles(False))


def _qkv_proj(x, gain, w, col_gain, col_mask, group, tm, tn, sub, n_norm):
    m, d = x.shape
    n = w.shape[1]
    return pl.pallas_call(
        functools.partial(_qkv_kernel, group=group, n_norm=n_norm, n_steps=n // tn, sub=sub),
        grid=(m // tm, n // tn),
        in_specs=[
            pl.BlockSpec((tm, d), lambda i, j: (i, 0)),
            pl.BlockSpec((1, d), lambda i, j: (0, 0)),
            pl.BlockSpec((d, tn), lambda i, j: (0, j)),
            pl.BlockSpec((1, tn), lambda i, j: (0, j)),
            pl.BlockSpec((1, tn), lambda i, j: (0, j)),
        ],
        out_specs=pl.BlockSpec((tm, tn), lambda i, j: (i, j)),
        out_shape=jax.ShapeDtypeStruct((m, n), BF16),
        scratch_shapes=[pltpu.VMEM((tm, d), BF16)],
        compiler_params=_params("parallel", "arbitrary"),
        name="qkv_proj",
    )(x, gain.reshape(1, d), w, col_gain.reshape(1, n), col_mask.reshape(1, n))


def _attn_a_kernel(slope_ref, sink_ref, hm_ref, q_ref, kvp_ref, kvc_ref, kvn_ref, *rest, seq, n_cast):
    w_refs, o_ref, wo_refs = rest[:n_cast], rest[n_cast], rest[n_cast + 1:]
    _cast_blocks(w_refs, wo_refs)
    kv_all = jnp.concatenate([kvp_ref[...], kvc_ref[...], kvn_ref[...]], axis=0)
    for u in range(ATTN_A_QBLOCKS):
        _attn_a_block(slope_ref, sink_ref, hm_ref, q_ref, o_ref, kv_all[u * BLK:u * BLK + BLK + 2 * WIN],
                      slice(u * BLK, (u + 1) * BLK), pl.program_id(1) * ATTN_A_QBLOCKS + u, seq)


def _attn_a_block(slope_ref, sink_ref, hm_ref, q_ref, o_ref, kv, rows, n, seq):
    span = BLK + 2 * WIN
    pairs = A_G // 2
    ti = lax.broadcasted_iota(jnp.int32, (BLK, span), 0)
    sj = lax.broadcasted_iota(jnp.int32, (BLK, span), 1)
    dist_i = jnp.abs(ti + WIN - sj)
    pos = sj + (n * BLK - WIN)
    valid = (dist_i <= WIN) & (pos >= 0) & (pos < seq)
    dist = jnp.where(valid, dist_i.astype(F32), MASKED_DIST)
    low_half = lax.broadcasted_iota(jnp.int32, (BLK, LANES), 1) < A_DH
    keep_lo = hm_ref[0:1, :]
    keep_hi = hm_ref[1:2, :]

    def block_diag(tile, head_in_low_half):
        swapped = jnp.concatenate([tile[:, A_DH:], tile[:, :A_DH]], axis=1)
        lo, hi = (tile, swapped) if head_in_low_half else (swapped, tile)
        return jnp.concatenate([lo * keep_lo, hi * keep_hi], axis=0)

    for kvh in range(A_KV):
        t0 = (kvh // 2) * LANES
        kbd = block_diag(kv[:, t0:t0 + LANES], kvh % 2 == 0)
        vbd = block_diag(kv[:, A_NK + t0:A_NK + t0 + LANES], kvh % 2 == 0)
        q0 = kvh * pairs * LANES
        qp = jnp.concatenate([q_ref[rows, q0 + pp * LANES:q0 + (pp + 1) * LANES] for pp in range(pairs)], axis=0)
        s = _dot_nt(qp, kbd)
        probs, scales = [], []
        for pp in range(pairs):
            halves, rls = [], []
            for e in range(2):
                h = kvh * A_G + 2 * pp + e
                sh = s[pp * BLK:(pp + 1) * BLK, e * span:(e + 1) * span] - slope_ref[h] * dist
                sink = sink_ref[h]
                mx = jnp.maximum(jnp.max(sh, axis=-1, keepdims=True), sink)
                p = jnp.exp2(sh - mx)
                l = jnp.sum(p, axis=-1, keepdims=True) + jnp.exp2(sink - mx)
                halves.append(p.astype(BF16))
                rls.append(1.0 / l)
            probs.append(jnp.concatenate(halves, axis=1))
            scales.append(jnp.where(low_half, rls[0], rls[1]))
        pv = _dot(jnp.concatenate(probs, axis=0), vbd)
        for pp in range(pairs):
            o_ref[rows, q0 + pp * LANES:q0 + (pp + 1) * LANES] = (
                pv[pp * BLK:(pp + 1) * BLK] * scales[pp]).astype(o_ref.dtype)


def _attn_a(qkv, slopes, sink, batch, seq, cast_weights):
    nb = seq // BLK
    qb = ATTN_A_QBLOCKS
    ns = nb // qb
    kv_col = A_NQ // (2 * A_NK)
    lane = jnp.arange(LANES)
    half_masks = jnp.stack([lane < A_DH, lane >= A_DH]).astype(BF16)
    plan = _cast_plan(cast_weights, batch * ns)
    cast_in, cast_out, cast_shapes = _cast_specs(plan, lambda b, n: b * ns + n)
    outs = pl.pallas_call(
        functools.partial(_attn_a_kernel, seq=seq, n_cast=len(plan)),
        grid=(batch, ns),
        in_specs=[
            pl.BlockSpec(memory_space=pltpu.SMEM),
            pl.BlockSpec(memory_space=pltpu.SMEM),
            pl.BlockSpec((2, LANES), lambda b, n: (0, 0)),
            pl.BlockSpec((qb * BLK, A_NQ), lambda b, n: (b * ns + n, 0)),
            pl.BlockSpec((BLK, 2 * A_NK), lambda b, n: (b * nb + jnp.maximum(n * qb - 1, 0), kv_col)),
            pl.BlockSpec((qb * BLK, 2 * A_NK), lambda b, n: (b * ns + n, kv_col)),
            pl.BlockSpec((BLK, 2 * A_NK), lambda b, n: (b * nb + jnp.minimum(n * qb + qb, nb - 1), kv_col)),
        ] + cast_in,
        out_specs=[pl.BlockSpec((qb * BLK, A_NQ), lambda b, n: (b * ns + n, 0))] + cast_out,
        out_shape=[jax.ShapeDtypeStruct((batch * seq, A_NQ), BF16)] + cast_shapes,
        compiler_params=_params("arbitrary", "arbitrary"),
        name="attn_a",
    )(slopes, sink, half_masks, qkv, qkv, qkv, qkv, *[p[0] for p in plan])
    return outs[0], outs[1:]


def _attn_b_block(t, batch, nq):
    t = jnp.clip(t, 0, B_H * batch * nq - 1)
    return t // (nq * batch), (t // nq) % batch, t % nq


def _attn_b_kernel(slope_ref, lam_ref, sub_ref, q_ref, k_ref, v_ref, *rest, lambda_init, batch, nq, n_cast):
    w_refs, o_ref, wo_refs = rest[:n_cast], rest[n_cast], rest[n_cast + 1:2 * n_cast + 1]
    bias_ref, s_ref, m_ref = rest[2 * n_cast + 1:]
    t = pl.program_id(0)
    tq = q_ref.shape[0]
    seq = k_ref.shape[0]
    ts = tq // ATTN_B_SUBS
    h, b, i = _attn_b_block(t, batch, nq)
    row0 = pl.multiple_of(i * tq, tq)

    @pl.when(t == 0)
    def _():
        s_ref[...] = jnp.zeros_like(s_ref)
        m_ref[...] = jnp.zeros_like(m_ref)

    @pl.when(b == 0)
    def _():
        tpos = (lax.broadcasted_iota(jnp.int32, (tq, seq), 0) + i * tq).astype(F32)
        spos = lax.broadcasted_iota(jnp.int32, (tq, seq), 1).astype(F32)
        bias_ref[pl.ds(row0, tq), :] = slope_ref[h] * jnp.abs(tpos - spos)

    lv = lam_ref[...]
    lam = (jnp.exp(jnp.sum(lv[0:1] * lv[1:2], axis=-1, keepdims=True))
           - jnp.exp(jnp.sum(lv[2:3] * lv[3:4], axis=-1, keepdims=True)) + lambda_init)

    _cast_blocks(w_refs, wo_refs)
    for u in range(ATTN_B_SUBS):
        sub_rows = slice(u * ts, (u + 1) * ts)
        bias_rows = pl.ds(row0 + u * ts, ts)
        pv = [None, None]
        l_part = [None, None]
        m_run = [None, None]
        for n in range(seq // ATTN_B_KCHUNK):
            k0 = n * ATTN_B_KCHUNK
            keys = slice(k0, k0 + ATTN_B_KCHUNK)
            for c in range(2):
                m_rep = m_ref[u, c]
                ps = [jnp.exp2(s_ref[u, c, :, k0 + j * LANES:k0 + (j + 1) * LANES] - m_rep)
                      for j in range(ATTN_B_KCHUNK // LANES)]
                l_new = functools.reduce(jnp.add, ps)
                l_part[c] = l_new if n == 0 else l_part[c] + l_new
                pv_new = _dot(jnp.concatenate(ps, axis=1).astype(BF16), v_ref[keys, :])
                pv[c] = pv_new if n == 0 else pv[c] + pv_new
            for c in range(2):
                dims = slice(c * B_DH, (c + 1) * B_DH)
                s = _dot_nt(q_ref[sub_rows, dims], k_ref[keys, dims]) - bias_ref[bias_rows, keys]
                s_ref[u, c, :, keys] = s
                part = functools.reduce(
                    jnp.maximum, [s[:, j * LANES:(j + 1) * LANES] for j in range(ATTN_B_KCHUNK // LANES)])
                m_run[c] = part if n == 0 else jnp.maximum(m_run[c], part)
        l1 = jnp.sum(l_part[0], axis=-1, keepdims=True)
        l2 = jnp.sum(l_part[1], axis=-1, keepdims=True)
        o = pv[0] * (1.0 / l1) - pv[1] * (lam / l2)
        o_ref[sub_rows, :] = (_row_rms(o, sub_ref[...]) * (1.0 - lambda_init)).astype(o_ref.dtype)
        for c in range(2):
            m_ref[u, c] = jnp.broadcast_to(jnp.max(m_run[c], axis=-1, keepdims=True), (ts, LANES))


def _attn_b(qkv, slopes, lam_vecs, subln, batch, seq, lambda_init, cast_weights):
    tq = ATTN_B_TQ
    ts = tq // ATTN_B_SUBS
    nq = seq // tq
    e = 2 * B_DH
    n_steps = B_H * batch * nq + 1

    def q_map(t):
        h, b, i = _attn_b_block(t, batch, nq)
        return b * nq + i, h

    def k_map(t):
        h, b, _ = _attn_b_block(t, batch, nq)
        return b, B_H + h

    def v_map(t):
        h, b, _ = _attn_b_block(t - 1, batch, nq)
        return b, 2 * B_H + h

    plan = _cast_plan(cast_weights, n_steps)
    cast_in, cast_out, cast_shapes = _cast_specs(plan, lambda t: t)
    outs = pl.pallas_call(
        functools.partial(_attn_b_kernel, lambda_init=lambda_init, batch=batch, nq=nq, n_cast=len(plan)),
        grid=(n_steps,),
        in_specs=[
            pl.BlockSpec(memory_space=pltpu.SMEM),
            pl.BlockSpec((4, B_DH), lambda t: (0, 0)),
            pl.BlockSpec((1, e), lambda t: (0, 0)),
            pl.BlockSpec((tq, e), q_map),
            pl.BlockSpec((seq, e), k_map),
            pl.BlockSpec((seq, e), v_map),
        ] + cast_in,
        out_specs=[pl.BlockSpec((tq, e), lambda t: q_map(t - 1))] + cast_out,
        out_shape=[jax.ShapeDtypeStruct((batch * seq, B_W), BF16)] + cast_shapes,
        scratch_shapes=[pltpu.VMEM((seq, seq), F32),
                        pltpu.VMEM((ATTN_B_SUBS, 2, ts, seq), F32),
                        pltpu.VMEM((ATTN_B_SUBS, 2, ts, LANES), F32)],
        compiler_params=_params("arbitrary"),
        name="attn_b",
    )(slopes, lam_vecs, subln.reshape(1, e), qkv, qkv, qkv, *[p[0] for p in plan])
    return outs[0], outs[1:]


def _proj_res_kernel(a_ref, w_ref, r_ref, o_ref):
    for c in range(o_ref.shape[1] // PROJ_SUB):
        cols = slice(c * PROJ_SUB, (c + 1) * PROJ_SUB)
        o_ref[:, cols] = r_ref[:, cols] + _dot(a_ref[...], w_ref[:, cols])


def _proj_res(a, w, res):
    m, kdim = a.shape
    n = w.shape[1]
    tm = PROJ_TM
    return pl.pallas_call(
        _proj_res_kernel,
        grid=(m // tm,),
        in_specs=[
            pl.BlockSpec((tm, kdim), lambda i: (i, 0)),
            pl.BlockSpec((kdim, n), lambda i: (0, 0)),
            pl.BlockSpec((tm, n), lambda i: (i, 0)),
        ],
        out_specs=pl.BlockSpec((tm, n), lambda i: (i, 0)),
        out_shape=jax.ShapeDtypeStruct((m, n), F32),
        compiler_params=_params("parallel"),
        name="proj_res",
    )(a, w, res)


def _ffn_kernel(x_ref, g_ref, wg_ref, wu_ref, wo_ref, o_ref, hn_ref):
    f = pl.program_id(1)

    def chunk(base_ref):
        hn = hn_ref[...]
        gate = _dot(hn, wg_ref[...])
        up = _dot(hn, wu_ref[...])
        act = (gate * jax.nn.sigmoid(gate) * up).astype(BF16)
        o_ref[...] = base_ref[...] + _dot(act, wo_ref[...])

    @pl.when(f == 0)
    def _():
        hn_ref[...] = _row_rms(x_ref[...], g_ref[...]).astype(BF16)
        chunk(x_ref)

    @pl.when(f > 0)
    def _():
        chunk(o_ref)


def _ffn(x, gain, w_in, w_out):
    m, d = x.shape
    tm, tf = FFN_TM, FFN_TF
    nf = D_FF // tf
    return pl.pallas_call(
        _ffn_kernel,
        grid=(m // tm, nf),
        in_specs=[
            pl.BlockSpec((tm, d), lambda i, f: (i, 0)),
            pl.BlockSpec((1, d), lambda i, f: (0, 0)),
            pl.BlockSpec((d, tf), lambda i, f: (0, f)),
            pl.BlockSpec((d, tf), lambda i, f: (0, nf + f)),
            pl.BlockSpec((tf, d), lambda i, f: (f, 0)),
        ],
        out_specs=pl.BlockSpec((tm, d), lambda i, f: (i, 0)),
        out_shape=jax.ShapeDtypeStruct((m, d), F32),
        scratch_shapes=[pltpu.VMEM((tm, d), BF16)],
        compiler_params=_params("parallel", "arbitrary"),
        name="ffn",
    )(x, gain.reshape(1, d), w_in, w_in, w_out)


def _ple_kernel(h_ref, p_ref, gn_ref, wg_ref, wp_ref, pn_ref, o_ref):
    hn = _row_rms(h_ref[...], gn_ref[...]).astype(BF16)
    pe = _row_rms(_dot(p_ref[...].astype(BF16), wp_ref[...]), pn_ref[...])
    for c in range(o_ref.shape[1] // PLE_SUB):
        cols = slice(c * PLE_SUB, (c + 1) * PLE_SUB)
        gate = jax.nn.sigmoid(_dot(hn, wg_ref[:, cols]))
        o_ref[:, cols] = h_ref[:, cols] + pe[:, cols] * gate


def _ple(h, p_all, layer, gate_norm, w_gate, w_proj, post_norm):
    m, d = h.shape
    tm = PLE_TM
    return pl.pallas_call(
        _ple_kernel,
        grid=(m // tm,),
        in_specs=[
            pl.BlockSpec((tm, d), lambda i: (i, 0)),
            pl.BlockSpec((None, tm, PLE_DIM), lambda i: (layer, i, 0)),
            pl.BlockSpec((1, d), lambda i: (0, 0)),
            pl.BlockSpec((d, d), lambda i: (0, 0)),
            pl.BlockSpec((PLE_DIM, d), lambda i: (0, 0)),
            pl.BlockSpec((1, d), lambda i: (0, 0)),
        ],
        out_specs=pl.BlockSpec((tm, d), lambda i: (i, 0)),
        out_shape=jax.ShapeDtypeStruct((m, d), F32),
        compiler_params=_params("parallel"),
        name="ple",
    )(h, p_all, gate_norm.reshape(1, d), w_gate, w_proj, post_norm.reshape(1, d))


def _alibi_slopes(n):
    return jnp.exp2(-8.0 * jnp.arange(1, n + 1, dtype=F32) / n)


def kernel(x, p, attn_norm, ffn_norm, a_w_qkv, a_q_norm, a_k_norm, a_sink, a_w_o, b_w_qkv, b_q_norm, b_k_norm, b_lambda, b_subln, b_w_o, w_ffn_in, w_ffn_out, ple_w_proj, ple_post_norm, ple_gate_norm, ple_w_gate):
    batch, seq, d = x.shape
    m = batch * seq
    h = x.reshape(m, d)
    p_all = p.reshape(DEPTH, m, PLE_DIM)
    w_qkv = a_w_qkv[0].astype(BF16)
    for i in range(DEPTH):
        j = i // 2
        if i % 2 == 0:
            col_gain = jnp.concatenate([jnp.tile(a_q_norm[j] * (A_DH ** -0.5 * LOG2E), A_HQ),
                                        jnp.tile(a_k_norm[j], A_KV), jnp.ones((A_NK,), F32)])
            col_mask = jnp.concatenate([jnp.ones((A_NQ + A_NK,), F32), jnp.zeros((A_NK,), F32)])
            qkv = _qkv_proj(h, attn_norm[i], w_qkv, col_gain, col_mask, A_DH,
                            QKV_A_TM, QKV_A_TN, QKV_A_SUB, A_QKV // QKV_A_TN)
            mix, (w_o, w_in, w_out, w_gate, w_proj, w_qkv) = _attn_a(
                qkv, _alibi_slopes(A_HQ) * LOG2E, a_sink[j].astype(F32) * LOG2E, batch, seq,
                [(a_w_o, j), (w_ffn_in, i), (w_ffn_out, i), (ple_w_gate, i), (ple_w_proj, i), (b_w_qkv, j)])
        else:
            lambda_init = 0.8 - 0.6 * math.exp(-0.3 * i)
            col_gain = jnp.concatenate([jnp.tile(b_q_norm[j] * (B_DH ** -0.5 * LOG2E), 2 * B_H),
                                        jnp.tile(b_k_norm[j], 2 * B_H), jnp.ones((B_W,), F32)])
            col_mask = jnp.concatenate([jnp.ones((2 * B_W,), F32), jnp.zeros((B_W,), F32)])
            qkv = _qkv_proj(h, attn_norm[i], w_qkv, col_gain, col_mask, B_DH,
                            QKV_B_TM, QKV_B_TN, QKV_B_SUB, 2 * B_W // QKV_B_TN)
            mix, (w_o, w_in, w_out, w_gate, w_proj) = _attn_b(
                qkv, _alibi_slopes(B_H) * LOG2E, b_lambda[j], b_subln[j], batch, seq, lambda_init,
                [(b_w_o, j), (w_ffn_in, i), (w_ffn_out, i), (ple_w_gate, i), (ple_w_proj, i)])
        h = _proj_res(mix, w_o, h)
        h = _ffn(h, ffn_norm[i], w_in, w_out)
        h = _ple(h, p_all, i, ple_gate_norm[i], w_gate, w_proj, ple_post_norm[i])
    return h.reshape(batch, seq, d)
```

```python
import functools
import math

import jax
import jax.numpy as jnp
from jax import lax
from jax.experimental import pallas as pl
from jax.experimental.pallas import tpu as pltpu

D_MODEL = 2048
DEPTH = 2
BLK = 128
WIN = 128
A_DH = 64
A_HQ = D_MODEL // A_DH
A_KV = 4
A_G = A_HQ // A_KV
A_NQ = A_HQ * A_DH
A_NK = A_KV * A_DH
A_QKV = A_NQ + 2 * A_NK
B_DH = 128
B_H = D_MODEL // (2 * B_DH)
B_W = B_H * 2 * B_DH
B_QKV = 3 * B_W
D_FF = int(math.ceil(8 * D_MODEL / 3 / 256) * 256)
PLE_DIM = 256
EPS = 1e-6
MASKED_DIST = 1e30
LOG2E = math.log2(math.e)

V7X_VMEM_LIMIT = 60 * 1024 * 1024
LANES = 128
MXU_DIM = 256
BF16_TILE_ROWS = 16
QKV_ROW_SPLIT = 2
QKV_A_TM, QKV_A_TN, QKV_A_SUB = 512, A_QKV, 512
QKV_B_TM, QKV_B_TN, QKV_B_SUB = 1024, B_W, 512
PROJ_TM, PROJ_SUB = 512, 512
FFN_TM, FFN_TF = 1024, 512
PLE_TM, PLE_SUB, PLE_ROW_SPLIT = 1024, 512, 4
ATTN_A_QBLOCKS = 2
ATTN_B_TQ, ATTN_B_SUBS, ATTN_B_KCHUNK = 1024, 4, 2 * LANES

F32 = jnp.float32
BF16 = jnp.bfloat16


def _params(*sem):
    return pltpu.CompilerParams(dimension_semantics=sem, vmem_limit_bytes=V7X_VMEM_LIMIT)


def _row_rms(x, gain):
    return x * lax.rsqrt(jnp.mean(x * x, axis=-1, keepdims=True) + EPS) * gain


def _dot(a, b):
    return jnp.dot(a, b, preferred_element_type=F32)


def _dot_nt(a, b):
    return lax.dot_general(a, b, (((1,), (1,)), ((), ())), preferred_element_type=F32)


def _cast_plan(weights, n_steps):
    plan = []
    for w_all, layer in weights:
        tiles = w_all.shape[1] // BF16_TILE_ROWS
        nblk = max(k for k in range(1, min(tiles, n_steps) + 1) if tiles % k == 0)
        plan.append((w_all, layer, nblk, w_all.shape[1] // nblk))
    return plan


def _cast_specs(plan, step_of):
    in_specs, out_specs, out_shapes = [], [], []
    for w_all, layer, nblk, rows in plan:
        cols = w_all.shape[2]
        in_specs.append(pl.BlockSpec(
            (None, rows, cols),
            lambda *ids, layer=layer, nblk=nblk: (layer, jnp.minimum(step_of(*ids), nblk - 1), 0)))
        out_specs.append(pl.BlockSpec(
            (rows, cols), lambda *ids, nblk=nblk: (jnp.minimum(step_of(*ids), nblk - 1), 0)))
        out_shapes.append(jax.ShapeDtypeStruct(w_all.shape[1:], BF16))
    return in_specs, out_specs, out_shapes


def _cast_blocks(w_refs, o_refs):
    for w_ref, o_ref in zip(w_refs, o_refs):
        o_ref[...] = w_ref[...].astype(BF16)


def _head_sumsq(tile, group):
    sq = tile * tile
    total = jnp.sum(sq, axis=-1, keepdims=True)
    if group == LANES:
        return total
    assert 2 * group == LANES
    low = lax.broadcasted_iota(jnp.int32, tile.shape, 1) < group
    low_sum = jnp.sum(jnp.where(low, sq, 0.0), axis=-1, keepdims=True)
    return jnp.where(low, low_sum, total - low_sum)


def _qkv_kernel(x_ref, g_ref, w_ref, cg_ref, cm_ref, o_ref, hn_ref, *, group, n_norm, n_steps, sub):
    j = pl.program_id(1)
    tn = o_ref.shape[1]

    def tiles(normed, rows=slice(None)):
        for c in range(tn // sub):
            cols = slice(c * sub, (c + 1) * sub)
            o = _dot(hn_ref[rows, :], w_ref[:, cols])
            if normed:
                parts = []
                for t in range(sub // LANES):
                    tile = o[:, t * LANES:(t + 1) * LANES]
                    parts.append(tile * lax.rsqrt(_head_sumsq(tile, group) * (1.0 / group) + EPS))
                nrm = jnp.concatenate(parts, axis=1) * cg_ref[:, cols]
                o = jnp.where(cm_ref[:, cols] > 0.0, nrm, o)
            o_ref[rows, cols] = o.astype(o_ref.dtype)

    @pl.when(j == 0)
    def _():
        piece = o_ref.shape[0] // QKV_ROW_SPLIT
        for u in range(QKV_ROW_SPLIT):
            rows = slice(u * piece, (u + 1) * piece)
            hn_ref[rows, :] = _row_rms(x_ref[rows, :], g_ref[...]).astype(BF16)
            tiles(n_norm > 0, rows)

    if n_norm > 1:
        pl.when((j > 0) & (j < n_norm))(lambda: tiles(True))
    if n_steps > max(n_norm, 1):
        pl.when(j >= max(n_norm, 1))(lambda: tiles(False))


def _qkv_proj(x, gain, w, col_gain, col_mask, group, tm, tn, sub, n_norm):
    m, d = x.shape
    n = w.shape[1]
    return pl.pallas_call(
        functools.partial(_qkv_kernel, group=group, n_norm=n_norm, n_steps=n // tn, sub=sub),
        grid=(m // tm, n // tn),
        in_specs=[
            pl.BlockSpec((tm, d), lambda i, j: (i, 0)),
            pl.BlockSpec((1, d), lambda i, j: (0, 0)),
            pl.BlockSpec((d, tn), lambda i, j: (0, j)),
            pl.BlockSpec((1, tn), lambda i, j: (0, j)),
            pl.BlockSpec((1, tn), lambda i, j: (0, j)),
        ],
        out_specs=pl.BlockSpec((tm, tn), lambda i, j: (i, j)),
        out_shape=jax.ShapeDtypeStruct((m, n), BF16),
        scratch_shapes=[pltpu.VMEM((tm, d), BF16)],
        compiler_params=_params("parallel", "arbitrary"),
        name="qkv_proj",
    )(x, gain.reshape(1, d), w, col_gain.reshape(1, n), col_mask.reshape(1, n))


def _attn_a_kernel(slope_ref, sink_ref, hm_ref, q_ref, kvp_ref, kvc_ref, kvn_ref, *rest, seq, n_cast):
    w_refs, o_ref, wo_refs = rest[:n_cast], rest[n_cast], rest[n_cast + 1:]
    _cast_blocks(w_refs, wo_refs)
    kv_all = jnp.concatenate([kvp_ref[...], kvc_ref[...], kvn_ref[...]], axis=0)
    for u in range(ATTN_A_QBLOCKS):
        _attn_a_block(slope_ref, sink_ref, hm_ref, q_ref, o_ref, kv_all[u * BLK:u * BLK + BLK + 2 * WIN],
                      slice(u * BLK, (u + 1) * BLK), pl.program_id(1) * ATTN_A_QBLOCKS + u, seq)


def _attn_a_block(slope_ref, sink_ref, hm_ref, q_ref, o_ref, kv, rows, n, seq):
    span = BLK + 2 * WIN
    pairs = A_G // 2
    ti = lax.broadcasted_iota(jnp.int32, (BLK, span), 0)
    sj = lax.broadcasted_iota(jnp.int32, (BLK, span), 1)
    dist_i = jnp.abs(ti + WIN - sj)
    pos = sj + (n * BLK - WIN)
    valid = (dist_i <= WIN) & (pos >= 0) & (pos < seq)
    dist = jnp.where(valid, dist_i.astype(F32), MASKED_DIST)
    low_half = lax.broadcasted_iota(jnp.int32, (BLK, LANES), 1) < A_DH
    keep_lo = hm_ref[0:1, :]
    keep_hi = hm_ref[1:2, :]

    def block_diag(tile, head_in_low_half):
        swapped = jnp.concatenate([tile[:, A_DH:], tile[:, :A_DH]], axis=1)
        lo, hi = (tile, swapped) if head_in_low_half else (swapped, tile)
        return jnp.concatenate([lo * keep_lo, hi * keep_hi], axis=0)

    for kvh in range(A_KV):
        t0 = (kvh // 2) * LANES
        kbd = block_diag(kv[:, t0:t0 + LANES], kvh % 2 == 0)
        vbd = block_diag(kv[:, A_NK + t0:A_NK + t0 + LANES], kvh % 2 == 0)
        q0 = kvh * pairs * LANES
        qp = jnp.concatenate([q_ref[rows, q0 + pp * LANES:q0 + (pp + 1) * LANES] for pp in range(pairs)], axis=0)
        s = _dot_nt(qp, kbd)
        probs, scales = [], []
        for pp in range(pairs):
            halves, rls = [], []
            for e in range(2):
                h = kvh * A_G + 2 * pp + e
                sh = s[pp * BLK:(pp + 1) * BLK, e * span:(e + 1) * span] - slope_ref[h] * dist
                sink = sink_ref[h]
                mx = jnp.maximum(jnp.max(sh, axis=-1, keepdims=True), sink)
                p = jnp.exp2(sh - mx)
                l = jnp.sum(p, axis=-1, keepdims=True) + jnp.exp2(sink - mx)
                halves.append(p.astype(BF16))
                rls.append(1.0 / l)
            probs.append(jnp.concatenate(halves, axis=1))
            scales.append(jnp.where(low_half, rls[0], rls[1]))
        pv = _dot(jnp.concatenate(probs, axis=0), vbd)
        for pp in range(pairs):
            o_ref[rows, q0 + pp * LANES:q0 + (pp + 1) * LANES] = (
                pv[pp * BLK:(pp + 1) * BLK] * scales[pp]).astype(o_ref.dtype)


def _attn_a(qkv, slopes, sink, batch, seq, cast_weights):
    nb = seq // BLK
    qb = ATTN_A_QBLOCKS
    ns = nb // qb
    kv_col = A_NQ // (2 * A_NK)
    lane = jnp.arange(LANES)
    half_masks = jnp.stack([lane < A_DH, lane >= A_DH]).astype(BF16)
    plan = _cast_plan(cast_weights, batch * ns)
    cast_in, cast_out, cast_shapes = _cast_specs(plan, lambda b, n: b * ns + n)
    outs = pl.pallas_call(
        functools.partial(_attn_a_kernel, seq=seq, n_cast=len(plan)),
        grid=(batch, ns),
        in_specs=[
            pl.BlockSpec(memory_space=pltpu.SMEM),
            pl.BlockSpec(memory_space=pltpu.SMEM),
            pl.BlockSpec((2, LANES), lambda b, n: (0, 0)),
            pl.BlockSpec((qb * BLK, A_NQ), lambda b, n: (b * ns + n, 0)),
            pl.BlockSpec((BLK, 2 * A_NK), lambda b, n: (b * nb + jnp.maximum(n * qb - 1, 0), kv_col)),
            pl.BlockSpec((qb * BLK, 2 * A_NK), lambda b, n: (b * ns + n, kv_col)),
            pl.BlockSpec((BLK, 2 * A_NK), lambda b, n: (b * nb + jnp.minimum(n * qb + qb, nb - 1), kv_col)),
        ] + cast_in,
        out_specs=[pl.BlockSpec((qb * BLK, A_NQ), lambda b, n: (b * ns + n, 0))] + cast_out,
        out_shape=[jax.ShapeDtypeStruct((batch * seq, A_NQ), BF16)] + cast_shapes,
        compiler_params=_params("arbitrary", "arbitrary"),
        name="attn_a",
    )(slopes, sink, half_masks, qkv, qkv, qkv, qkv, *[p[0] for p in plan])
    return outs[0], outs[1:]


def _attn_b_block(t, batch, nq):
    t = jnp.clip(t, 0, B_H * batch * nq - 1)
    return t // (nq * batch), (t // nq) % batch, t % nq


def _attn_b_kernel(slope_ref, lam_ref, sub_ref, q_ref, k_ref, v_ref, *rest, lambda_init, batch, nq, n_cast):
    w_refs, o_ref, wo_refs = rest[:n_cast], rest[n_cast], rest[n_cast + 1:2 * n_cast + 1]
    bias_ref, s_ref, m_ref = rest[2 * n_cast + 1:]
    t = pl.program_id(0)
    tq = q_ref.shape[0]
    seq = k_ref.shape[0]
    ts = tq // ATTN_B_SUBS
    h, b, i = _attn_b_block(t, batch, nq)
    row0 = pl.multiple_of(i * tq, tq)

    @pl.when(t == 0)
    def _():
        s_ref[...] = jnp.zeros_like(s_ref)
        m_ref[...] = jnp.zeros_like(m_ref)

    @pl.when(b == 0)
    def _():
        tpos = (lax.broadcasted_iota(jnp.int32, (tq, seq), 0) + i * tq).astype(F32)
        spos = lax.broadcasted_iota(jnp.int32, (tq, seq), 1).astype(F32)
        bias_ref[pl.ds(row0, tq), :] = slope_ref[h] * jnp.abs(tpos - spos)

    lv = lam_ref[...]
    lam = (jnp.exp(jnp.sum(lv[0:1] * lv[1:2], axis=-1, keepdims=True))
           - jnp.exp(jnp.sum(lv[2:3] * lv[3:4], axis=-1, keepdims=True)) + lambda_init)

    _cast_blocks(w_refs, wo_refs)
    for u in range(ATTN_B_SUBS):
        sub_rows = slice(u * ts, (u + 1) * ts)
        bias_rows = pl.ds(row0 + u * ts, ts)
        pv = [None, None]
        l_part = [None, None]
        m_run = [None, None]
        for n in range(seq // ATTN_B_KCHUNK):
            k0 = n * ATTN_B_KCHUNK
            keys = slice(k0, k0 + ATTN_B_KCHUNK)
            for c in range(2):
                m_rep = m_ref[u, c]
                ps = [jnp.exp2(s_ref[u, c, :, k0 + j * LANES:k0 + (j + 1) * LANES] - m_rep)
                      for j in range(ATTN_B_KCHUNK // LANES)]
                l_new = functools.reduce(jnp.add, ps)
                l_part[c] = l_new if n == 0 else l_part[c] + l_new
                pv_new = _dot(jnp.concatenate(ps, axis=1).astype(BF16), v_ref[keys, :])
                pv[c] = pv_new if n == 0 else pv[c] + pv_new
            for c in range(2):
                dims = slice(c * B_DH, (c + 1) * B_DH)
                s = _dot_nt(q_ref[sub_rows, dims], k_ref[keys, dims]) - bias_ref[bias_rows, keys]
                s_ref[u, c, :, keys] = s
                part = functools.reduce(
                    jnp.maximum, [s[:, j * LANES:(j + 1) * LANES] for j in range(ATTN_B_KCHUNK // LANES)])
                m_run[c] = part if n == 0 else jnp.maximum(m_run[c], part)
        l1 = jnp.sum(l_part[0], axis=-1, keepdims=True)
        l2 = jnp.sum(l_part[1], axis=-1, keepdims=True)
        o = pv[0] * (1.0 / l1) - pv[1] * (lam / l2)
        o_ref[sub_rows, :] = (_row_rms(o, sub_ref[...]) * (1.0 - lambda_init)).astype(o_ref.dtype)
        for c in range(2):
            m_ref[u, c] = jnp.broadcast_to(jnp.max(m_run[c], axis=-1, keepdims=True), (ts, LANES))


def _attn_b(qkv, slopes, lam_vecs, subln, batch, seq, lambda_init, cast_weights):
    tq = ATTN_B_TQ
    ts = tq // ATTN_B_SUBS
    nq = seq // tq
    e = 2 * B_DH
    n_steps = B_H * batch * nq + 1

    def q_map(t):
        h, b, i = _attn_b_block(t, batch, nq)
        return b * nq + i, h

    def k_map(t):
        h, b, _ = _attn_b_block(t, batch, nq)
        return b, B_H + h

    def v_map(t):
        h, b, _ = _attn_b_block(t - 1, batch, nq)
        return b, 2 * B_H + h

    plan = _cast_plan(cast_weights, n_steps)
    cast_in, cast_out, cast_shapes = _cast_specs(plan, lambda t: t)
    outs = pl.pallas_call(
        functools.partial(_attn_b_kernel, lambda_init=lambda_init, batch=batch, nq=nq, n_cast=len(plan)),
        grid=(n_steps,),
        in_specs=[
            pl.BlockSpec(memory_space=pltpu.SMEM),
            pl.BlockSpec((4, B_DH), lambda t: (0, 0)),
            pl.BlockSpec((1, e), lambda t: (0, 0)),
            pl.BlockSpec((tq, e), q_map),
            pl.BlockSpec((seq, e), k_map),
            pl.BlockSpec((seq, e), v_map),
        ] + cast_in,
        out_specs=[pl.BlockSpec((tq, e), lambda t: q_map(t - 1))] + cast_out,
        out_shape=[jax.ShapeDtypeStruct((batch * seq, B_W), BF16)] + cast_shapes,
        scratch_shapes=[pltpu.VMEM((seq, seq), F32),
                        pltpu.VMEM((ATTN_B_SUBS, 2, ts, seq), F32),
                        pltpu.VMEM((ATTN_B_SUBS, 2, ts, LANES), F32)],
        compiler_params=_params("arbitrary"),
        name="attn_b",
    )(slopes, lam_vecs, subln.reshape(1, e), qkv, qkv, qkv, *[p[0] for p in plan])
    return outs[0], outs[1:]


def _proj_res_kernel(a_ref, w_ref, r_ref, o_ref):
    for c in range(o_ref.shape[1] // PROJ_SUB):
        cols = slice(c * PROJ_SUB, (c + 1) * PROJ_SUB)
        o_ref[:, cols] = r_ref[:, cols] + _dot(a_ref[...], w_ref[:, cols])


def _proj_res(a, w, res):
    m, kdim = a.shape
    n = w.shape[1]
    tm = PROJ_TM
    return pl.pallas_call(
        _proj_res_kernel,
        grid=(m // tm,),
        in_specs=[
            pl.BlockSpec((tm, kdim), lambda i: (i, 0)),
            pl.BlockSpec((kdim, n), lambda i: (0, 0)),
            pl.BlockSpec((tm, n), lambda i: (i, 0)),
        ],
        out_specs=pl.BlockSpec((tm, n), lambda i: (i, 0)),
        out_shape=jax.ShapeDtypeStruct((m, n), F32),
        compiler_params=_params("parallel"),
        name="proj_res",
    )(a, w, res)


def _ffn_kernel(x_ref, g_ref, wg_ref, wu_ref, wo_ref, o_ref, hn_ref):
    f = pl.program_id(1)

    def chunk(base_ref):
        hn = hn_ref[...]
        gate = _dot(hn, wg_ref[...])
        up = _dot(hn, wu_ref[...])
        act = (gate * jax.nn.sigmoid(gate) * up).astype(BF16)
        o_ref[...] = base_ref[...] + _dot(act, wo_ref[...])

    @pl.when(f == 0)
    def _():
        hn_ref[...] = _row_rms(x_ref[...], g_ref[...]).astype(BF16)
        chunk(x_ref)

    @pl.when(f > 0)
    def _():
        chunk(o_ref)


def _ffn(x, gain, w_in, w_out):
    m, d = x.shape
    tm, tf = FFN_TM, FFN_TF
    nf = D_FF // tf
    return pl.pallas_call(
        _ffn_kernel,
        grid=(m // tm, nf),
        in_specs=[
            pl.BlockSpec((tm, d), lambda i, f: (i, 0)),
            pl.BlockSpec((1, d), lambda i, f: (0, 0)),
            pl.BlockSpec((d, tf), lambda i, f: (0, f)),
            pl.BlockSpec((d, tf), lambda i, f: (0, nf + f)),
            pl.BlockSpec((tf, d), lambda i, f: (f, 0)),
        ],
        out_specs=pl.BlockSpec((tm, d), lambda i, f: (i, 0)),
        out_shape=jax.ShapeDtypeStruct((m, d), F32),
        scratch_shapes=[pltpu.VMEM((tm, d), BF16)],
        compiler_params=_params("parallel", "arbitrary"),
        name="ffn",
    )(x, gain.reshape(1, d), w_in, w_in, w_out)


def _ple_kernel(h_ref, p_ref, gn_ref, wg_ref, wp_ref, pn_ref, o_ref):
    for u in range(PLE_ROW_SPLIT):
        rows = slice(u * (o_ref.shape[0] // PLE_ROW_SPLIT), (u + 1) * (o_ref.shape[0] // PLE_ROW_SPLIT))
        hn = _row_rms(h_ref[rows, :], gn_ref[...]).astype(BF16)
        pe = _row_rms(_dot(p_ref[rows, :].astype(BF16), wp_ref[...]), pn_ref[...])
        for c in range(o_ref.shape[1] // PLE_SUB):
            cols = slice(c * PLE_SUB, (c + 1) * PLE_SUB)
            gate = jax.nn.sigmoid(_dot(hn, wg_ref[:, cols]))
            o_ref[rows, cols] = h_ref[rows, cols] + pe[:, cols] * gate


def _ple(h, p_all, layer, gate_norm, w_gate, w_proj, post_norm):
    m, d = h.shape
    tm = PLE_TM
    return pl.pallas_call(
        _ple_kernel,
        grid=(m // tm,),
        in_specs=[
            pl.BlockSpec((tm, d), lambda i: (i, 0)),
            pl.BlockSpec((None, tm, PLE_DIM), lambda i: (layer, i, 0)),
            pl.BlockSpec((1, d), lambda i: (0, 0)),
            pl.BlockSpec((d, d), lambda i: (0, 0)),
            pl.BlockSpec((PLE_DIM, d), lambda i: (0, 0)),
            pl.BlockSpec((1, d), lambda i: (0, 0)),
        ],
        out_specs=pl.BlockSpec((tm, d), lambda i: (i, 0)),
        out_shape=jax.ShapeDtypeStruct((m, d), F32),
        compiler_params=_params("parallel"),
        name="ple",
    )(h, p_all, gate_norm.reshape(1, d), w_gate, w_proj, post_norm.reshape(1, d))


def _alibi_slopes(n):
    return jnp.exp2(-8.0 * jnp.arange(1, n + 1, dtype=F32) / n)


def kernel(x, p, attn_norm, ffn_norm, a_w_qkv, a_q_norm, a_k_norm, a_sink, a_w_o, b_w_qkv, b_q_norm, b_k_norm, b_lambda, b_subln, b_w_o, w_ffn_in, w_ffn_out, ple_w_proj, ple_post_norm, ple_gate_norm, ple_w_gate):
    batch, seq, d = x.shape
    m = batch * seq
    h = x.reshape(m, d)
    p_all = p.reshape(DEPTH, m, PLE_DIM)
    w_qkv = a_w_qkv[0].astype(BF16)
    for i in range(DEPTH):
        j = i // 2
        if i % 2 == 0:
            col_gain = jnp.concatenate([jnp.tile(a_q_norm[j] * (A_DH ** -0.5 * LOG2E), A_HQ),
                                        jnp.tile(a_k_norm[j], A_KV), jnp.ones((A_NK,), F32)])
            col_mask = jnp.concatenate([jnp.ones((A_NQ + A_NK,), F32), jnp.zeros((A_NK,), F32)])
            qkv = _qkv_proj(h, attn_norm[i], w_qkv, col_gain, col_mask, A_DH,
                            QKV_A_TM, QKV_A_TN, QKV_A_SUB, A_QKV // QKV_A_TN)
            mix, (w_o, w_in, w_out, w_gate, w_proj, w_qkv) = _attn_a(
                qkv, _alibi_slopes(A_HQ) * LOG2E, a_sink[j].astype(F32) * LOG2E, batch, seq,
                [(a_w_o, j), (w_ffn_in, i), (w_ffn_out, i), (ple_w_gate, i), (ple_w_proj, i), (b_w_qkv, j)])
        else:
            lambda_init = 0.8 - 0.6 * math.exp(-0.3 * i)
            col_gain = jnp.concatenate([jnp.tile(b_q_norm[j] * (B_DH ** -0.5 * LOG2E), 2 * B_H),
                                        jnp.tile(b_k_norm[j], 2 * B_H), jnp.ones((B_W,), F32)])
            col_mask = jnp.concatenate([jnp.ones((2 * B_W,), F32), jnp.zeros((B_W,), F32)])
            qkv = _qkv_proj(h, attn_norm[i], w_qkv, col_gain, col_mask, B_DH,
                            QKV_B_TM, QKV_B_TN, QKV_B_SUB, 2 * B_W // QKV_B_TN)
            mix, (w_o, w_in, w_out, w_gate, w_proj) = _attn_b(
                qkv, _alibi_slopes(B_H) * LOG2E, b_lambda[j], b_subln[j], batch, seq, lambda_init,
                [(b_w_o, j), (w_ffn_in, i), (w_ffn_out, i), (ple_w_gate, i), (ple_w_proj, i)])
        h = _proj_res(mix, w_o, h)
        h = _ffn(h, ffn_norm[i], w_in, w_out)
        h = _ple(h, p_all, i, ple_gate_norm[i], w_gate, w_proj, ple_post_norm[i])
    return h.reshape(batch, seq, d)
```

```python
import functools
import math

import jax
import jax.numpy as jnp
from jax import lax
from jax.experimental import pallas as pl
from jax.experimental.pallas import tpu as pltpu

D_MODEL = 2048
DEPTH = 2
BLK = 128
WIN = 128
A_DH = 64
A_HQ = D_MODEL // A_DH
A_KV = 4
A_G = A_HQ // A_KV
A_NQ = A_HQ * A_DH
A_NK = A_KV * A_DH
A_QKV = A_NQ + 2 * A_NK
B_DH = 128
B_H = D_MODEL // (2 * B_DH)
B_W = B_H * 2 * B_DH
B_QKV = 3 * B_W
D_FF = int(math.ceil(8 * D_MODEL / 3 / 256) * 256)
PLE_DIM = 256
EPS = 1e-6
MASKED_DIST = 1e30
LOG2E = math.log2(math.e)

V7X_VMEM_LIMIT = 60 * 1024 * 1024
LANES = 128
MXU_DIM = 256
BF16_TILE_ROWS = 16
QKV_ROW_SPLIT = 2
QKV_A_TM, QKV_A_TN, QKV_A_SUB = 512, A_QKV, 512
QKV_B_TM, QKV_B_TN, QKV_B_SUB = 1024, B_W, 512
PROJ_TM, PROJ_SUB = 512, 512
FFN_TM, FFN_TF = 1024, 512
PLE_TM, PLE_SUB, PLE_ROW_SPLIT = 1024, 512, 4
ATTN_A_QBLOCKS = 4
ATTN_B_TQ, ATTN_B_SUBS, ATTN_B_KCHUNK = 1024, 4, 2 * LANES

F32 = jnp.float32
BF16 = jnp.bfloat16


def _params(*sem):
    return pltpu.CompilerParams(dimension_semantics=sem, vmem_limit_bytes=V7X_VMEM_LIMIT)


def _row_rms(x, gain):
    return x * lax.rsqrt(jnp.mean(x * x, axis=-1, keepdims=True) + EPS) * gain


def _dot(a, b):
    return jnp.dot(a, b, preferred_element_type=F32)


def _dot_nt(a, b):
    return lax.dot_general(a, b, (((1,), (1,)), ((), ())), preferred_element_type=F32)


def _cast_plan(weights, n_steps):
    plan = []
    for w_all, layer, chunked in weights:
        tiles = w_all.shape[1] // BF16_TILE_ROWS
        nblk = max(k for k in range(1, min(tiles, n_steps) + 1) if tiles % k == 0)
        plan.append((w_all, layer, nblk, w_all.shape[1] // nblk, chunked))
    return plan


def _cast_specs(plan, step_of):
    in_specs, out_specs, out_shapes = [], [], []
    for w_all, layer, nblk, rows, chunked in plan:
        _, total_rows, cols = w_all.shape
        in_specs.append(pl.BlockSpec(
            (None, rows, cols),
            lambda *ids, layer=layer, nblk=nblk: (layer, jnp.minimum(step_of(*ids), nblk - 1), 0)))
        if chunked:
            nf = cols // (2 * FFN_TF)
            out_specs.append(pl.BlockSpec(
                (nf, rows, 2 * FFN_TF), lambda *ids, nblk=nblk: (0, jnp.minimum(step_of(*ids), nblk - 1), 0)))
            out_shapes.append(jax.ShapeDtypeStruct((nf, total_rows, 2 * FFN_TF), BF16))
        else:
            out_specs.append(pl.BlockSpec(
                (rows, cols), lambda *ids, nblk=nblk: (jnp.minimum(step_of(*ids), nblk - 1), 0)))
            out_shapes.append(jax.ShapeDtypeStruct((total_rows, cols), BF16))
    return in_specs, out_specs, out_shapes


def _cast_blocks(w_refs, o_refs):
    for w_ref, o_ref in zip(w_refs, o_refs):
        if len(o_ref.shape) == 3:
            nf = o_ref.shape[0]
            for f in range(nf):
                o_ref[f, :, :FFN_TF] = w_ref[:, f * FFN_TF:(f + 1) * FFN_TF].astype(BF16)
                o_ref[f, :, FFN_TF:] = w_ref[:, (nf + f) * FFN_TF:(nf + f + 1) * FFN_TF].astype(BF16)
        else:
            o_ref[...] = w_ref[...].astype(BF16)


def _head_sumsq(tile, group):
    sq = tile * tile
    total = jnp.sum(sq, axis=-1, keepdims=True)
    if group == LANES:
        return total
    assert 2 * group == LANES
    low = lax.broadcasted_iota(jnp.int32, tile.shape, 1) < group
    low_sum = jnp.sum(jnp.where(low, sq, 0.0), axis=-1, keepdims=True)
    return jnp.where(low, low_sum, total - low_sum)


def _qkv_kernel(x_ref, g_ref, w_ref, cg_ref, cm_ref, o_ref, hn_ref, *, group, n_norm, n_steps, sub):
    j = pl.program_id(1)
    tn = o_ref.shape[1]

    def tiles(normed, rows=slice(None)):
        for c in range(tn // sub):
            cols = slice(c * sub, (c + 1) * sub)
            o = _dot(hn_ref[rows, :], w_ref[:, cols])
            if normed:
                parts = []
                for t in range(sub // LANES):
                    tile = o[:, t * LANES:(t + 1) * LANES]
                    parts.append(tile * lax.rsqrt(_head_sumsq(tile, group) * (1.0 / group) + EPS))
                nrm = jnp.concatenate(parts, axis=1) * cg_ref[:, cols]
                o = jnp.where(cm_ref[:, cols] > 0.0, nrm, o)
            o_ref[rows, cols] = o.astype(o_ref.dtype)

    @pl.when(j == 0)
    def _():
        piece = o_ref.shape[0] // QKV_ROW_SPLIT
        for u in range(QKV_ROW_SPLIT):
            rows = slice(u * piece, (u + 1) * piece)
            hn_ref[rows, :] = _row_rms(x_ref[rows, :], g_ref[...]).astype(BF16)
            tiles(n_norm > 0, rows)

    if n_norm > 1:
        pl.when((j > 0) & (j < n_norm))(lambda: tiles(True))
    if n_steps > max(n_norm, 1):
        pl.when(j >= max(n_norm, 1))(lambda: tiles(False))


def _qkv_proj(x, gain, w, col_gain, col_mask, group, tm, tn, sub, n_norm):
    m, d = x.shape
    n = w.shape[1]
    return pl.pallas_call(
        functools.partial(_qkv_kernel, group=group, n_norm=n_norm, n_steps=n // tn, sub=sub),
        grid=(m // tm, n // tn),
        in_specs=[
            pl.BlockSpec((tm, d), lambda i, j: (i, 0)),
            pl.BlockSpec((1, d), lambda i, j: (0, 0)),
            pl.BlockSpec((d, tn), lambda i, j: (0, j)),
            pl.BlockSpec((1, tn), lambda i, j: (0, j)),
            pl.BlockSpec((1, tn), lambda i, j: (0, j)),
        ],
        out_specs=pl.BlockSpec((tm, tn), lambda i, j: (i, j)),
        out_shape=jax.ShapeDtypeStruct((m, n), BF16),
        scratch_shapes=[pltpu.VMEM((tm, d), BF16)],
        compiler_params=_params("parallel", "arbitrary"),
        name="qkv_proj",
    )(x, gain.reshape(1, d), w, col_gain.reshape(1, n), col_mask.reshape(1, n))


def _attn_a_kernel(slope_ref, sink_ref, hm_ref, q_ref, kvp_ref, kvc_ref, kvn_ref, *rest, seq, n_cast):
    w_refs, o_ref, wo_refs = rest[:n_cast], rest[n_cast], rest[n_cast + 1:]
    _cast_blocks(w_refs, wo_refs)
    kv_all = jnp.concatenate([kvp_ref[...], kvc_ref[...], kvn_ref[...]], axis=0)
    for u in range(ATTN_A_QBLOCKS):
        _attn_a_block(slope_ref, sink_ref, hm_ref, q_ref, o_ref, kv_all[u * BLK:u * BLK + BLK + 2 * WIN],
                      slice(u * BLK, (u + 1) * BLK), pl.program_id(1) * ATTN_A_QBLOCKS + u, seq)


def _attn_a_block(slope_ref, sink_ref, hm_ref, q_ref, o_ref, kv, rows, n, seq):
    span = BLK + 2 * WIN
    pairs = A_G // 2
    ti = lax.broadcasted_iota(jnp.int32, (BLK, span), 0)
    sj = lax.broadcasted_iota(jnp.int32, (BLK, span), 1)
    dist_i = jnp.abs(ti + WIN - sj)
    pos = sj + (n * BLK - WIN)
    valid = (dist_i <= WIN) & (pos >= 0) & (pos < seq)
    dist = jnp.where(valid, dist_i.astype(F32), MASKED_DIST)
    low_half = lax.broadcasted_iota(jnp.int32, (BLK, LANES), 1) < A_DH
    keep_lo = hm_ref[0:1, :]
    keep_hi = hm_ref[1:2, :]

    def block_diag(tile, head_in_low_half):
        swapped = jnp.concatenate([tile[:, A_DH:], tile[:, :A_DH]], axis=1)
        lo, hi = (tile, swapped) if head_in_low_half else (swapped, tile)
        return jnp.concatenate([lo * keep_lo, hi * keep_hi], axis=0)

    for kvh in range(A_KV):
        t0 = (kvh // 2) * LANES
        kbd = block_diag(kv[:, t0:t0 + LANES], kvh % 2 == 0)
        vbd = block_diag(kv[:, A_NK + t0:A_NK + t0 + LANES], kvh % 2 == 0)
        q0 = kvh * pairs * LANES
        qp = jnp.concatenate([q_ref[rows, q0 + pp * LANES:q0 + (pp + 1) * LANES] for pp in range(pairs)], axis=0)
        s = _dot_nt(qp, kbd)
        probs, scales = [], []
        for pp in range(pairs):
            halves, rls = [], []
            for e in range(2):
                h = kvh * A_G + 2 * pp + e
                sh = s[pp * BLK:(pp + 1) * BLK, e * span:(e + 1) * span] - slope_ref[h] * dist
                sink = sink_ref[h]
                mx = jnp.maximum(jnp.max(sh, axis=-1, keepdims=True), sink)
                p = jnp.exp2(sh - mx)
                l = jnp.sum(p, axis=-1, keepdims=True) + jnp.exp2(sink - mx)
                halves.append(p.astype(BF16))
                rls.append(1.0 / l)
            probs.append(jnp.concatenate(halves, axis=1))
            scales.append(jnp.where(low_half, rls[0], rls[1]))
        pv = _dot(jnp.concatenate(probs, axis=0), vbd)
        for pp in range(pairs):
            o_ref[rows, q0 + pp * LANES:q0 + (pp + 1) * LANES] = (
                pv[pp * BLK:(pp + 1) * BLK] * scales[pp]).astype(o_ref.dtype)


def _attn_a(qkv, slopes, sink, batch, seq, cast_weights):
    nb = seq // BLK
    qb = ATTN_A_QBLOCKS
    ns = nb // qb
    kv_col = A_NQ // (2 * A_NK)
    lane = jnp.arange(LANES)
    half_masks = jnp.stack([lane < A_DH, lane >= A_DH]).astype(BF16)
    plan = _cast_plan(cast_weights, batch * ns)
    cast_in, cast_out, cast_shapes = _cast_specs(plan, lambda b, n: b * ns + n)
    outs = pl.pallas_call(
        functools.partial(_attn_a_kernel, seq=seq, n_cast=len(plan)),
        grid=(batch, ns),
        in_specs=[
            pl.BlockSpec(memory_space=pltpu.SMEM),
            pl.BlockSpec(memory_space=pltpu.SMEM),
            pl.BlockSpec((2, LANES), lambda b, n: (0, 0)),
            pl.BlockSpec((qb * BLK, A_NQ), lambda b, n: (b * ns + n, 0)),
            pl.BlockSpec((BLK, 2 * A_NK), lambda b, n: (b * nb + jnp.maximum(n * qb - 1, 0), kv_col)),
            pl.BlockSpec((qb * BLK, 2 * A_NK), lambda b, n: (b * ns + n, kv_col)),
            pl.BlockSpec((BLK, 2 * A_NK), lambda b, n: (b * nb + jnp.minimum(n * qb + qb, nb - 1), kv_col)),
        ] + cast_in,
        out_specs=[pl.BlockSpec((qb * BLK, A_NQ), lambda b, n: (b * ns + n, 0))] + cast_out,
        out_shape=[jax.ShapeDtypeStruct((batch * seq, A_NQ), BF16)] + cast_shapes,
        compiler_params=_params("arbitrary", "arbitrary"),
        name="attn_a",
    )(slopes, sink, half_masks, qkv, qkv, qkv, qkv, *[p[0] for p in plan])
    return outs[0], outs[1:]


def _attn_b_block(t, batch, nq):
    t = jnp.clip(t, 0, B_H * batch * nq - 1)
    return t // (nq * batch), (t // nq) % batch, t % nq


def _attn_b_kernel(slope_ref, lam_ref, sub_ref, q_ref, k_ref, v_ref, *rest, lambda_init, batch, nq, n_cast):
    w_refs, o_ref, wo_refs = rest[:n_cast], rest[n_cast], rest[n_cast + 1:2 * n_cast + 1]
    bias_ref, s_ref, m_ref = rest[2 * n_cast + 1:]
    t = pl.program_id(0)
    tq = q_ref.shape[0]
    seq = k_ref.shape[0]
    ts = tq // ATTN_B_SUBS
    h, b, i = _attn_b_block(t, batch, nq)
    row0 = pl.multiple_of(i * tq, tq)

    @pl.when(t == 0)
    def _():
        s_ref[...] = jnp.zeros_like(s_ref)
        m_ref[...] = jnp.zeros_like(m_ref)

    @pl.when(b == 0)
    def _():
        tpos = (lax.broadcasted_iota(jnp.int32, (tq, seq), 0) + i * tq).astype(F32)
        spos = lax.broadcasted_iota(jnp.int32, (tq, seq), 1).astype(F32)
        bias_ref[pl.ds(row0, tq), :] = slope_ref[h] * jnp.abs(tpos - spos)

    lv = lam_ref[...]
    lam = (jnp.exp(jnp.sum(lv[0:1] * lv[1:2], axis=-1, keepdims=True))
           - jnp.exp(jnp.sum(lv[2:3] * lv[3:4], axis=-1, keepdims=True)) + lambda_init)

    _cast_blocks(w_refs, wo_refs)
    for u in range(ATTN_B_SUBS):
        sub_rows = slice(u * ts, (u + 1) * ts)
        bias_rows = pl.ds(row0 + u * ts, ts)
        pv = [None, None]
        l_part = [None, None]
        m_run = [None, None]
        for n in range(seq // ATTN_B_KCHUNK):
            k0 = n * ATTN_B_KCHUNK
            keys = slice(k0, k0 + ATTN_B_KCHUNK)
            for c in range(2):
                m_rep = m_ref[u, c]
                ps = [jnp.exp2(s_ref[u, c, :, k0 + j * LANES:k0 + (j + 1) * LANES] - m_rep)
                      for j in range(ATTN_B_KCHUNK // LANES)]
                l_new = functools.reduce(jnp.add, ps)
                l_part[c] = l_new if n == 0 else l_part[c] + l_new
                pv_new = _dot(jnp.concatenate(ps, axis=1).astype(BF16), v_ref[keys, :])
                pv[c] = pv_new if n == 0 else pv[c] + pv_new
            for c in range(2):
                dims = slice(c * B_DH, (c + 1) * B_DH)
                s = _dot_nt(q_ref[sub_rows, dims], k_ref[keys, dims]) - bias_ref[bias_rows, keys]
                s_ref[u, c, :, keys] = s
                part = functools.reduce(
                    jnp.maximum, [s[:, j * LANES:(j + 1) * LANES] for j in range(ATTN_B_KCHUNK // LANES)])
                m_run[c] = part if n == 0 else jnp.maximum(m_run[c], part)
        l1 = jnp.sum(l_part[0], axis=-1, keepdims=True)
        l2 = jnp.sum(l_part[1], axis=-1, keepdims=True)
        o = pv[0] * (1.0 / l1) - pv[1] * (lam / l2)
        o_ref[sub_rows, :] = (_row_rms(o, sub_ref[...]) * (1.0 - lambda_init)).astype(o_ref.dtype)
        for c in range(2):
            m_ref[u, c] = jnp.broadcast_to(jnp.max(m_run[c], axis=-1, keepdims=True), (ts, LANES))


def _attn_b(qkv, slopes, lam_vecs, subln, batch, seq, lambda_init, cast_weights):
    tq = ATTN_B_TQ
    ts = tq // ATTN_B_SUBS
    nq = seq // tq
    e = 2 * B_DH
    n_steps = B_H * batch * nq + 1

    def q_map(t):
        h, b, i = _attn_b_block(t, batch, nq)
        return b * nq + i, h

    def k_map(t):
        h, b, _ = _attn_b_block(t, batch, nq)
        return b, B_H + h

    def v_map(t):
        h, b, _ = _attn_b_block(t - 1, batch, nq)
        return b, 2 * B_H + h

    plan = _cast_plan(cast_weights, n_steps)
    cast_in, cast_out, cast_shapes = _cast_specs(plan, lambda t: t)
    outs = pl.pallas_call(
        functools.partial(_attn_b_kernel, lambda_init=lambda_init, batch=batch, nq=nq, n_cast=len(plan)),
        grid=(n_steps,),
        in_specs=[
            pl.BlockSpec(memory_space=pltpu.SMEM),
            pl.BlockSpec((4, B_DH), lambda t: (0, 0)),
            pl.BlockSpec((1, e), lambda t: (0, 0)),
            pl.BlockSpec((tq, e), q_map),
            pl.BlockSpec((seq, e), k_map),
            pl.BlockSpec((seq, e), v_map),
        ] + cast_in,
        out_specs=[pl.BlockSpec((tq, e), lambda t: q_map(t - 1))] + cast_out,
        out_shape=[jax.ShapeDtypeStruct((batch * seq, B_W), BF16)] + cast_shapes,
        scratch_shapes=[pltpu.VMEM((seq, seq), F32),
                        pltpu.VMEM((ATTN_B_SUBS, 2, ts, seq), F32),
                        pltpu.VMEM((ATTN_B_SUBS, 2, ts, LANES), F32)],
        compiler_params=_params("arbitrary"),
        name="attn_b",
    )(slopes, lam_vecs, subln.reshape(1, e), qkv, qkv, qkv, *[p[0] for p in plan])
    return outs[0], outs[1:]


def _proj_res_kernel(a_ref, w_ref, r_ref, o_ref):
    for c in range(o_ref.shape[1] // PROJ_SUB):
        cols = slice(c * PROJ_SUB, (c + 1) * PROJ_SUB)
        o_ref[:, cols] = r_ref[:, cols] + _dot(a_ref[...], w_ref[:, cols])


def _proj_res(a, w, res):
    m, kdim = a.shape
    n = w.shape[1]
    tm = PROJ_TM
    return pl.pallas_call(
        _proj_res_kernel,
        grid=(m // tm,),
        in_specs=[
            pl.BlockSpec((tm, kdim), lambda i: (i, 0)),
            pl.BlockSpec((kdim, n), lambda i: (0, 0)),
            pl.BlockSpec((tm, n), lambda i: (i, 0)),
        ],
        out_specs=pl.BlockSpec((tm, n), lambda i: (i, 0)),
        out_shape=jax.ShapeDtypeStruct((m, n), F32),
        compiler_params=_params("parallel"),
        name="proj_res",
    )(a, w, res)


def _ffn_kernel(x_ref, g_ref, wgu_ref, wo_ref, o_ref, hn_ref):
    f = pl.program_id(1)
    tf = wo_ref.shape[0]

    def chunk(base_ref):
        gu = _dot(hn_ref[...], wgu_ref[...])
        gate = gu[:, :tf]
        act = (gate * jax.nn.sigmoid(gate) * gu[:, tf:]).astype(BF16)
        o_ref[...] = base_ref[...] + _dot(act, wo_ref[...])

    @pl.when(f == 0)
    def _():
        hn_ref[...] = _row_rms(x_ref[...], g_ref[...]).astype(BF16)
        chunk(x_ref)

    @pl.when(f > 0)
    def _():
        chunk(o_ref)


def _ffn(x, gain, w_in_chunks, w_out):
    m, d = x.shape
    tm, tf = FFN_TM, FFN_TF
    nf = D_FF // tf
    return pl.pallas_call(
        _ffn_kernel,
        grid=(m // tm, nf),
        in_specs=[
            pl.BlockSpec((tm, d), lambda i, f: (i, 0)),
            pl.BlockSpec((1, d), lambda i, f: (0, 0)),
            pl.BlockSpec((None, d, 2 * tf), lambda i, f: (f, 0, 0)),
            pl.BlockSpec((tf, d), lambda i, f: (f, 0)),
        ],
        out_specs=pl.BlockSpec((tm, d), lambda i, f: (i, 0)),
        out_shape=jax.ShapeDtypeStruct((m, d), F32),
        scratch_shapes=[pltpu.VMEM((tm, d), BF16)],
        compiler_params=_params("parallel", "arbitrary"),
        name="ffn",
    )(x, gain.reshape(1, d), w_in_chunks, w_out)


def _ple_kernel(h_ref, p_ref, gn_ref, wg_ref, wp_ref, pn_ref, o_ref):
    for u in range(PLE_ROW_SPLIT):
        rows = slice(u * (o_ref.shape[0] // PLE_ROW_SPLIT), (u + 1) * (o_ref.shape[0] // PLE_ROW_SPLIT))
        hn = _row_rms(h_ref[rows, :], gn_ref[...]).astype(BF16)
        pe = _row_rms(_dot(p_ref[rows, :].astype(BF16), wp_ref[...]), pn_ref[...])
        for c in range(o_ref.shape[1] // PLE_SUB):
            cols = slice(c * PLE_SUB, (c + 1) * PLE_SUB)
            gate = jax.nn.sigmoid(_dot(hn, wg_ref[:, cols]))
            o_ref[rows, cols] = h_ref[rows, cols] + pe[:, cols] * gate


def _ple(h, p_all, layer, gate_norm, w_gate, w_proj, post_norm):
    m, d = h.shape
    tm = PLE_TM
    return pl.pallas_call(
        _ple_kernel,
        grid=(m // tm,),
        in_specs=[
            pl.BlockSpec((tm, d), lambda i: (i, 0)),
            pl.BlockSpec((None, tm, PLE_DIM), lambda i: (layer, i, 0)),
            pl.BlockSpec((1, d), lambda i: (0, 0)),
            pl.BlockSpec((d, d), lambda i: (0, 0)),
            pl.BlockSpec((PLE_DIM, d), lambda i: (0, 0)),
            pl.BlockSpec((1, d), lambda i: (0, 0)),
        ],
        out_specs=pl.BlockSpec((tm, d), lambda i: (i, 0)),
        out_shape=jax.ShapeDtypeStruct((m, d), F32),
        compiler_params=_params("parallel"),
        name="ple",
    )(h, p_all, gate_norm.reshape(1, d), w_gate, w_proj, post_norm.reshape(1, d))


def _alibi_slopes(n):
    return jnp.exp2(-8.0 * jnp.arange(1, n + 1, dtype=F32) / n)


def kernel(x, p, attn_norm, ffn_norm, a_w_qkv, a_q_norm, a_k_norm, a_sink, a_w_o, b_w_qkv, b_q_norm, b_k_norm, b_lambda, b_subln, b_w_o, w_ffn_in, w_ffn_out, ple_w_proj, ple_post_norm, ple_gate_norm, ple_w_gate):
    batch, seq, d = x.shape
    m = batch * seq
    h = x.reshape(m, d)
    p_all = p.reshape(DEPTH, m, PLE_DIM)
    w_qkv = a_w_qkv[0].astype(BF16)
    for i in range(DEPTH):
        j = i // 2
        if i % 2 == 0:
            col_gain = jnp.concatenate([jnp.tile(a_q_norm[j] * (A_DH ** -0.5 * LOG2E), A_HQ),
                                        jnp.tile(a_k_norm[j], A_KV), jnp.ones((A_NK,), F32)])
            col_mask = jnp.concatenate([jnp.ones((A_NQ + A_NK,), F32), jnp.zeros((A_NK,), F32)])
            qkv = _qkv_proj(h, attn_norm[i], w_qkv, col_gain, col_mask, A_DH,
                            QKV_A_TM, QKV_A_TN, QKV_A_SUB, A_QKV // QKV_A_TN)
            mix, (w_o, w_in, w_out, w_gate, w_proj, w_qkv) = _attn_a(
                qkv, _alibi_slopes(A_HQ) * LOG2E, a_sink[j].astype(F32) * LOG2E, batch, seq,
                [(a_w_o, j, False), (w_ffn_in, i, True), (w_ffn_out, i, False), (ple_w_gate, i, False),
                 (ple_w_proj, i, False), (b_w_qkv, j, False)])
        else:
            lambda_init = 0.8 - 0.6 * math.exp(-0.3 * i)
            col_gain = jnp.concatenate([jnp.tile(b_q_norm[j] * (B_DH ** -0.5 * LOG2E), 2 * B_H),
                                        jnp.tile(b_k_norm[j], 2 * B_H), jnp.ones((B_W,), F32)])
            col_mask = jnp.concatenate([jnp.ones((2 * B_W,), F32), jnp.zeros((B_W,), F32)])
            qkv = _qkv_proj(h, attn_norm[i], w_qkv, col_gain, col_mask, B_DH,
                            QKV_B_TM, QKV_B_TN, QKV_B_SUB, 2 * B_W // QKV_B_TN)
            mix, (w_o, w_in, w_out, w_gate, w_proj) = _attn_b(
                qkv, _alibi_slopes(B_H) * LOG2E, b_lambda[j], b_subln[j], batch, seq, lambda_init,
                [(b_w_o, j, False), (w_ffn_in, i, True), (w_ffn_out, i, False), (ple_w_gate, i, False),
                 (ple_w_proj, i, False)])
        h = _proj_res(mix, w_o, h)
        h = _ffn(h, ffn_norm[i], w_in, w_out)
        h = _ple(h, p_all, i, ple_gate_norm[i], w_gate, w_proj, ple_post_norm[i])
    return h.reshape(batch, seq, d)
```

```python
import functools
import math

import jax
import jax.numpy as jnp
from jax import lax
from jax.experimental import pallas as pl
from jax.experimental.pallas import tpu as pltpu

D_MODEL = 2048
DEPTH = 2
BLK = 128
WIN = 128
A_DH = 64
A_HQ = D_MODEL // A_DH
A_KV = 4
A_G = A_HQ // A_KV
A_NQ = A_HQ * A_DH
A_NK = A_KV * A_DH
A_QKV = A_NQ + 2 * A_NK
B_DH = 128
B_H = D_MODEL // (2 * B_DH)
B_W = B_H * 2 * B_DH
B_QKV = 3 * B_W
D_FF = int(math.ceil(8 * D_MODEL / 3 / 256) * 256)
PLE_DIM = 256
EPS = 1e-6
MASKED_DIST = 1e30
LOG2E = math.log2(math.e)

V7X_VMEM_LIMIT = 60 * 1024 * 1024
LANES = 128
BF16_TILE_ROWS = 16
QKV_ROW_SPLIT = 2
QKV_A_TM, QKV_A_TN, QKV_A_SUB = 512, A_QKV, 512
QKV_B_TM, QKV_B_TN, QKV_B_SUB = 1024, B_W, 512
PROJ_TM, PROJ_SUB = 512, 512
FFN_TM, FFN_TF = 1024, 512
PLE_TM, PLE_SUB, PLE_ROW_SPLIT = 1024, 512, 4
ATTN_A_QBLOCKS = 4
ATTN_B_TQ, ATTN_B_SUBS, ATTN_B_KCHUNK = 1024, 4, 2 * LANES

F32 = jnp.float32
BF16 = jnp.bfloat16


def _params(*sem):
    return pltpu.CompilerParams(dimension_semantics=sem, vmem_limit_bytes=V7X_VMEM_LIMIT)


def _row_rms(x, gain):
    return x * lax.rsqrt(jnp.mean(x * x, axis=-1, keepdims=True) + EPS) * gain


def _dot(a, b):
    return jnp.dot(a, b, preferred_element_type=F32)


def _dot_nt(a, b):
    return lax.dot_general(a, b, (((1,), (1,)), ((), ())), preferred_element_type=F32)


def _cast_plan(weights, n_steps):
    plan = []
    for w_all, layer, chunked in weights:
        tiles = w_all.shape[1] // BF16_TILE_ROWS
        nblk = max(k for k in range(1, min(tiles, n_steps) + 1) if tiles % k == 0)
        plan.append((w_all, layer, nblk, w_all.shape[1] // nblk, chunked))
    return plan


def _cast_specs(plan, step_of):
    in_specs, out_specs, out_shapes = [], [], []
    for w_all, layer, nblk, rows, chunked in plan:
        _, total_rows, cols = w_all.shape
        in_specs.append(pl.BlockSpec(
            (None, rows, cols),
            lambda *ids, layer=layer, nblk=nblk: (layer, jnp.minimum(step_of(*ids), nblk - 1), 0)))
        if chunked:
            nf = cols // (2 * FFN_TF)
            out_specs.append(pl.BlockSpec(
                (nf, rows, 2 * FFN_TF), lambda *ids, nblk=nblk: (0, jnp.minimum(step_of(*ids), nblk - 1), 0)))
            out_shapes.append(jax.ShapeDtypeStruct((nf, total_rows, 2 * FFN_TF), BF16))
        else:
            out_specs.append(pl.BlockSpec(
                (rows, cols), lambda *ids, nblk=nblk: (jnp.minimum(step_of(*ids), nblk - 1), 0)))
            out_shapes.append(jax.ShapeDtypeStruct((total_rows, cols), BF16))
    return in_specs, out_specs, out_shapes


def _cast_blocks(w_refs, o_refs):
    for w_ref, o_ref in zip(w_refs, o_refs):
        if len(o_ref.shape) == 3:
            nf = o_ref.shape[0]
            for f in range(nf):
                o_ref[f, :, :FFN_TF] = w_ref[:, f * FFN_TF:(f + 1) * FFN_TF].astype(BF16)
                o_ref[f, :, FFN_TF:] = w_ref[:, (nf + f) * FFN_TF:(nf + f + 1) * FFN_TF].astype(BF16)
        else:
            o_ref[...] = w_ref[...].astype(BF16)


def _head_sumsq(tile, group):
    sq = tile * tile
    total = jnp.sum(sq, axis=-1, keepdims=True)
    if group == LANES:
        return total
    assert 2 * group == LANES
    low = lax.broadcasted_iota(jnp.int32, tile.shape, 1) < group
    low_sum = jnp.sum(jnp.where(low, sq, 0.0), axis=-1, keepdims=True)
    return jnp.where(low, low_sum, total - low_sum)


def _qkv_kernel(x_ref, g_ref, w_ref, cg_ref, cm_ref, o_ref, hn_ref, *, group, n_norm, n_steps, sub):
    j = pl.program_id(1)
    tn = o_ref.shape[1]

    def tiles(normed, rows=slice(None)):
        for c in range(tn // sub):
            cols = slice(c * sub, (c + 1) * sub)
            o = _dot(hn_ref[rows, :], w_ref[:, cols])
            if normed:
                parts = []
                for t in range(sub // LANES):
                    tile = o[:, t * LANES:(t + 1) * LANES]
                    parts.append(tile * lax.rsqrt(_head_sumsq(tile, group) * (1.0 / group) + EPS))
                nrm = jnp.concatenate(parts, axis=1) * cg_ref[:, cols]
                o = jnp.where(cm_ref[:, cols] > 0.0, nrm, o)
            o_ref[rows, cols] = o.astype(o_ref.dtype)

    @pl.when(j == 0)
    def _():
        piece = o_ref.shape[0] // QKV_ROW_SPLIT
        for u in range(QKV_ROW_SPLIT):
            rows = slice(u * piece, (u + 1) * piece)
            hn_ref[rows, :] = _row_rms(x_ref[rows, :], g_ref[...]).astype(BF16)
            tiles(n_norm > 0, rows)

    if n_norm > 1:
        pl.when((j > 0) & (j < n_norm))(lambda: tiles(True))
    if n_steps > max(n_norm, 1):
        pl.when(j >= max(n_norm, 1))(lambda: tiles(False))


def _qkv_proj(x, gain, w, col_gain, col_mask, group, tm, tn, sub, n_norm):
    m, d = x.shape
    n = w.shape[1]
    return pl.pallas_call(
        functools.partial(_qkv_kernel, group=group, n_norm=n_norm, n_steps=n // tn, sub=sub),
        grid=(m // tm, n // tn),
        in_specs=[
            pl.BlockSpec((tm, d), lambda i, j: (i, 0)),
            pl.BlockSpec((1, d), lambda i, j: (0, 0)),
            pl.BlockSpec((d, tn), lambda i, j: (0, j)),
            pl.BlockSpec((1, tn), lambda i, j: (0, j)),
            pl.BlockSpec((1, tn), lambda i, j: (0, j)),
        ],
        out_specs=pl.BlockSpec((tm, tn), lambda i, j: (i, j)),
        out_shape=jax.ShapeDtypeStruct((m, n), BF16),
        scratch_shapes=[pltpu.VMEM((tm, d), BF16)],
        compiler_params=_params("parallel", "arbitrary"),
        name="qkv_proj",
    )(x, gain.reshape(1, d), w, col_gain.reshape(1, n), col_mask.reshape(1, n))


def _attn_a_kernel(slope_ref, sink_ref, hm_ref, q_ref, kvp_ref, kvc_ref, kvn_ref, *rest, seq, n_cast):
    w_refs, o_ref, wo_refs = rest[:n_cast], rest[n_cast], rest[n_cast + 1:]
    _cast_blocks(w_refs, wo_refs)
    kv_all = jnp.concatenate([kvp_ref[...], kvc_ref[...], kvn_ref[...]], axis=0)
    for u in range(ATTN_A_QBLOCKS):
        _attn_a_block(slope_ref, sink_ref, hm_ref, q_ref, o_ref, kv_all[u * BLK:u * BLK + BLK + 2 * WIN],
                      slice(u * BLK, (u + 1) * BLK), pl.program_id(1) * ATTN_A_QBLOCKS + u, seq)


def _attn_a_block(slope_ref, sink_ref, hm_ref, q_ref, o_ref, kv, rows, n, seq):
    span = BLK + 2 * WIN
    pairs = A_G // 2
    ti = lax.broadcasted_iota(jnp.int32, (BLK, span), 0)
    sj = lax.broadcasted_iota(jnp.int32, (BLK, span), 1)
    dist_i = jnp.abs(ti + WIN - sj)
    pos = sj + (n * BLK - WIN)
    valid = (dist_i <= WIN) & (pos >= 0) & (pos < seq)
    dist = jnp.where(valid, dist_i.astype(F32), MASKED_DIST)
    low_half = lax.broadcasted_iota(jnp.int32, (BLK, LANES), 1) < A_DH
    keep_lo = hm_ref[0:1, :]
    keep_hi = hm_ref[1:2, :]

    def block_diag(tile, head_in_low_half):
        swapped = jnp.concatenate([tile[:, A_DH:], tile[:, :A_DH]], axis=1)
        lo, hi = (tile, swapped) if head_in_low_half else (swapped, tile)
        return jnp.concatenate([lo * keep_lo, hi * keep_hi], axis=0)

    for kvh in range(A_KV):
        t0 = (kvh // 2) * LANES
        kbd = block_diag(kv[:, t0:t0 + LANES], kvh % 2 == 0)
        vbd = block_diag(kv[:, A_NK + t0:A_NK + t0 + LANES], kvh % 2 == 0)
        q0 = kvh * pairs * LANES
        qp = jnp.concatenate([q_ref[rows, q0 + pp * LANES:q0 + (pp + 1) * LANES] for pp in range(pairs)], axis=0)
        s = _dot_nt(qp, kbd)
        probs, scales = [], []
        for pp in range(pairs):
            halves, rls = [], []
            for e in range(2):
                h = kvh * A_G + 2 * pp + e
                sh = s[pp * BLK:(pp + 1) * BLK, e * span:(e + 1) * span] - slope_ref[h] * dist
                sink = sink_ref[h]
                mx = jnp.maximum(jnp.max(sh, axis=-1, keepdims=True), sink)
                p = jnp.exp2(sh - mx)
                l = jnp.sum(p, axis=-1, keepdims=True) + jnp.exp2(sink - mx)
                halves.append(p.astype(BF16))
                rls.append(1.0 / l)
            probs.append(jnp.concatenate(halves, axis=1))
            scales.append(jnp.where(low_half, rls[0], rls[1]))
        pv = _dot(jnp.concatenate(probs, axis=0), vbd)
        for pp in range(pairs):
            o_ref[rows, q0 + pp * LANES:q0 + (pp + 1) * LANES] = (
                pv[pp * BLK:(pp + 1) * BLK] * scales[pp]).astype(o_ref.dtype)


def _attn_a(qkv, slopes, sink, batch, seq, cast_weights):
    nb = seq // BLK
    qb = ATTN_A_QBLOCKS
    ns = nb // qb
    kv_col = A_NQ // (2 * A_NK)
    lane = jnp.arange(LANES)
    half_masks = jnp.stack([lane < A_DH, lane >= A_DH]).astype(BF16)
    plan = _cast_plan(cast_weights, batch * ns)
    cast_in, cast_out, cast_shapes = _cast_specs(plan, lambda b, n: b * ns + n)
    outs = pl.pallas_call(
        functools.partial(_attn_a_kernel, seq=seq, n_cast=len(plan)),
        grid=(batch, ns),
        in_specs=[
            pl.BlockSpec(memory_space=pltpu.SMEM),
            pl.BlockSpec(memory_space=pltpu.SMEM),
            pl.BlockSpec((2, LANES), lambda b, n: (0, 0)),
            pl.BlockSpec((qb * BLK, A_NQ), lambda b, n: (b * ns + n, 0)),
            pl.BlockSpec((BLK, 2 * A_NK), lambda b, n: (b * nb + jnp.maximum(n * qb - 1, 0), kv_col)),
            pl.BlockSpec((qb * BLK, 2 * A_NK), lambda b, n: (b * ns + n, kv_col)),
            pl.BlockSpec((BLK, 2 * A_NK), lambda b, n: (b * nb + jnp.minimum(n * qb + qb, nb - 1), kv_col)),
        ] + cast_in,
        out_specs=[pl.BlockSpec((qb * BLK, A_NQ), lambda b, n: (b * ns + n, 0))] + cast_out,
        out_shape=[jax.ShapeDtypeStruct((batch * seq, A_NQ), BF16)] + cast_shapes,
        compiler_params=_params("arbitrary", "arbitrary"),
        name="attn_a",
    )(slopes, sink, half_masks, qkv, qkv, qkv, qkv, *[p[0] for p in plan])
    return outs[0], outs[1:]


def _attn_b_block(t, batch, nq):
    t = jnp.clip(t, 0, B_H * batch * nq - 1)
    return t // (nq * batch), (t // nq) % batch, t % nq


def _attn_b_kernel(slope_ref, lam_ref, sub_ref, q_ref, k_ref, v_ref, *rest, lambda_init, batch, nq, n_cast):
    w_refs, o_ref, wo_refs = rest[:n_cast], rest[n_cast], rest[n_cast + 1:2 * n_cast + 1]
    bias_ref, s_ref, m_ref = rest[2 * n_cast + 1:]
    t = pl.program_id(0)
    tq = q_ref.shape[0]
    seq = k_ref.shape[0]
    ts = tq // ATTN_B_SUBS
    h, b, i = _attn_b_block(t, batch, nq)
    row0 = pl.multiple_of(i * tq, tq)

    @pl.when(t == 0)
    def _():
        s_ref[...] = jnp.zeros_like(s_ref)
        m_ref[...] = jnp.zeros_like(m_ref)

    @pl.when(b == 0)
    def _():
        tpos = (lax.broadcasted_iota(jnp.int32, (tq, seq), 0) + i * tq).astype(F32)
        spos = lax.broadcasted_iota(jnp.int32, (tq, seq), 1).astype(F32)
        bias_ref[pl.ds(row0, tq), :] = slope_ref[h] * jnp.abs(tpos - spos)

    lv = lam_ref[...]
    lam = (jnp.exp(jnp.sum(lv[0:1] * lv[1:2], axis=-1, keepdims=True))
           - jnp.exp(jnp.sum(lv[2:3] * lv[3:4], axis=-1, keepdims=True)) + lambda_init)

    _cast_blocks(w_refs, wo_refs)
    for u in range(ATTN_B_SUBS):
        sub_rows = slice(u * ts, (u + 1) * ts)
        bias_rows = pl.ds(row0 + u * ts, ts)
        pv = [None, None]
        l_part = [None, None]
        m_run = [None, None]
        for n in range(seq // ATTN_B_KCHUNK):
            k0 = n * ATTN_B_KCHUNK
            keys = slice(k0, k0 + ATTN_B_KCHUNK)
            for c in range(2):
                m_rep = m_ref[u, c]
                ps = [jnp.exp2(s_ref[u, c, :, k0 + j * LANES:k0 + (j + 1) * LANES] - m_rep)
                      for j in range(ATTN_B_KCHUNK // LANES)]
                l_new = functools.reduce(jnp.add, ps)
                l_part[c] = l_new if n == 0 else l_part[c] + l_new
                pv_new = _dot(jnp.concatenate(ps, axis=1).astype(BF16), v_ref[keys, :])
                pv[c] = pv_new if n == 0 else pv[c] + pv_new
            for c in range(2):
                dims = slice(c * B_DH, (c + 1) * B_DH)
                s = _dot_nt(q_ref[sub_rows, dims], k_ref[keys, dims]) - bias_ref[bias_rows, keys]
                s_ref[u, c, :, keys] = s
                part = functools.reduce(
                    jnp.maximum, [s[:, j * LANES:(j + 1) * LANES] for j in range(ATTN_B_KCHUNK // LANES)])
                m_run[c] = part if n == 0 else jnp.maximum(m_run[c], part)
        l1 = jnp.sum(l_part[0], axis=-1, keepdims=True)
        l2 = jnp.sum(l_part[1], axis=-1, keepdims=True)
        o = pv[0] * (1.0 / l1) - pv[1] * (lam / l2)
        o_ref[sub_rows, :] = (_row_rms(o, sub_ref[...]) * (1.0 - lambda_init)).astype(o_ref.dtype)
        for c in range(2):
            m_ref[u, c] = jnp.broadcast_to(jnp.max(m_run[c], axis=-1, keepdims=True), (ts, LANES))


def _attn_b(qkv, slopes, lam_vecs, subln, batch, seq, lambda_init, cast_weights):
    tq = ATTN_B_TQ
    ts = tq // ATTN_B_SUBS
    nq = seq // tq
    e = 2 * B_DH
    n_steps = B_H * batch * nq + 1

    def q_map(t):
        h, b, i = _attn_b_block(t, batch, nq)
        return b * nq + i, h

    def k_map(t):
        h, b, _ = _attn_b_block(t, batch, nq)
        return b, B_H + h

    def v_map(t):
        h, b, _ = _attn_b_block(t - 1, batch, nq)
        return b, 2 * B_H + h

    plan = _cast_plan(cast_weights, n_steps)
    cast_in, cast_out, cast_shapes = _cast_specs(plan, lambda t: t)
    outs = pl.pallas_call(
        functools.partial(_attn_b_kernel, lambda_init=lambda_init, batch=batch, nq=nq, n_cast=len(plan)),
        grid=(n_steps,),
        in_specs=[
            pl.BlockSpec(memory_space=pltpu.SMEM),
            pl.BlockSpec((4, B_DH), lambda t: (0, 0)),
            pl.BlockSpec((1, e), lambda t: (0, 0)),
            pl.BlockSpec((tq, e), q_map),
            pl.BlockSpec((seq, e), k_map),
            pl.BlockSpec((seq, e), v_map),
        ] + cast_in,
        out_specs=[pl.BlockSpec((tq, e), lambda t: q_map(t - 1))] + cast_out,
        out_shape=[jax.ShapeDtypeStruct((batch * seq, B_W), BF16)] + cast_shapes,
        scratch_shapes=[pltpu.VMEM((seq, seq), F32),
                        pltpu.VMEM((ATTN_B_SUBS, 2, ts, seq), F32),
                        pltpu.VMEM((ATTN_B_SUBS, 2, ts, LANES), F32)],
        compiler_params=_params("arbitrary"),
        name="attn_b",
    )(slopes, lam_vecs, subln.reshape(1, e), qkv, qkv, qkv, *[p[0] for p in plan])
    return outs[0], outs[1:]


def _proj_res_kernel(a_ref, w_ref, r_ref, o_ref):
    for c in range(o_ref.shape[1] // PROJ_SUB):
        cols = slice(c * PROJ_SUB, (c + 1) * PROJ_SUB)
        o_ref[:, cols] = r_ref[:, cols] + _dot(a_ref[...], w_ref[:, cols])


def _proj_res(a, w, res):
    m, kdim = a.shape
    n = w.shape[1]
    tm = PROJ_TM
    return pl.pallas_call(
        _proj_res_kernel,
        grid=(m // tm,),
        in_specs=[
            pl.BlockSpec((tm, kdim), lambda i: (i, 0)),
            pl.BlockSpec((kdim, n), lambda i: (0, 0)),
            pl.BlockSpec((tm, n), lambda i: (i, 0)),
        ],
        out_specs=pl.BlockSpec((tm, n), lambda i: (i, 0)),
        out_shape=jax.ShapeDtypeStruct((m, n), F32),
        compiler_params=_params("parallel"),
        name="proj_res",
    )(a, w, res)


def _ffn_kernel(x_ref, g_ref, wgu_ref, wo_ref, *rest, n_cast):
    cast_in, o_ref, cast_out, hn_ref = rest[:n_cast], rest[n_cast], rest[n_cast + 1:2 * n_cast + 1], rest[-1]
    f = pl.program_id(1)
    tf = wo_ref.shape[0]

    def chunk(base_ref):
        _cast_blocks(cast_in, cast_out)
        gu = _dot(hn_ref[...], wgu_ref[...])
        gate = gu[:, :tf]
        act = (gate * jax.nn.sigmoid(gate) * gu[:, tf:]).astype(BF16)
        o_ref[...] = base_ref[...] + _dot(act, wo_ref[...])

    @pl.when(f == 0)
    def _():
        hn_ref[...] = _row_rms(x_ref[...], g_ref[...]).astype(BF16)
        chunk(x_ref)

    @pl.when(f > 0)
    def _():
        chunk(o_ref)


def _ffn(x, gain, w_in_chunks, w_out, cast_weights=()):
    m, d = x.shape
    tm, tf = FFN_TM, FFN_TF
    nf = D_FF // tf
    plan = _cast_plan(cast_weights, (m // tm) * nf)
    cast_in, cast_out, cast_shapes = _cast_specs(plan, lambda i, f: i * nf + f)
    outs = pl.pallas_call(
        functools.partial(_ffn_kernel, n_cast=len(plan)),
        grid=(m // tm, nf),
        in_specs=[
            pl.BlockSpec((tm, d), lambda i, f: (i, 0)),
            pl.BlockSpec((1, d), lambda i, f: (0, 0)),
            pl.BlockSpec((None, d, 2 * tf), lambda i, f: (f, 0, 0)),
            pl.BlockSpec((tf, d), lambda i, f: (f, 0)),
        ] + cast_in,
        out_specs=[pl.BlockSpec((tm, d), lambda i, f: (i, 0))] + cast_out,
        out_shape=[jax.ShapeDtypeStruct((m, d), F32)] + cast_shapes,
        scratch_shapes=[pltpu.VMEM((tm, d), BF16)],
        compiler_params=_params("arbitrary", "arbitrary"),
        name="ffn",
    )(x, gain.reshape(1, d), w_in_chunks, w_out, *[p[0] for p in plan])
    return outs[0], outs[1:]


def _ple_kernel(h_ref, p_ref, gn_ref, wg_ref, wp_ref, pn_ref, o_ref):
    for u in range(PLE_ROW_SPLIT):
        rows = slice(u * (o_ref.shape[0] // PLE_ROW_SPLIT), (u + 1) * (o_ref.shape[0] // PLE_ROW_SPLIT))
        hn = _row_rms(h_ref[rows, :], gn_ref[...]).astype(BF16)
        pe = _row_rms(_dot(p_ref[rows, :].astype(BF16), wp_ref[...]), pn_ref[...])
        for c in range(o_ref.shape[1] // PLE_SUB):
            cols = slice(c * PLE_SUB, (c + 1) * PLE_SUB)
            gate = jax.nn.sigmoid(_dot(hn, wg_ref[:, cols]))
            o_ref[rows, cols] = h_ref[rows, cols] + pe[:, cols] * gate


def _ple(h, p_all, layer, gate_norm, w_gate, w_proj, post_norm):
    m, d = h.shape
    tm = PLE_TM
    return pl.pallas_call(
        _ple_kernel,
        grid=(m // tm,),
        in_specs=[
            pl.BlockSpec((tm, d), lambda i: (i, 0)),
            pl.BlockSpec((None, tm, PLE_DIM), lambda i: (layer, i, 0)),
            pl.BlockSpec((1, d), lambda i: (0, 0)),
            pl.BlockSpec((d, d), lambda i: (0, 0)),
            pl.BlockSpec((PLE_DIM, d), lambda i: (0, 0)),
            pl.BlockSpec((1, d), lambda i: (0, 0)),
        ],
        out_specs=pl.BlockSpec((tm, d), lambda i: (i, 0)),
        out_shape=jax.ShapeDtypeStruct((m, d), F32),
        compiler_params=_params("parallel"),
        name="ple",
    )(h, p_all, gate_norm.reshape(1, d), w_gate, w_proj, post_norm.reshape(1, d))


def _alibi_slopes(n):
    return jnp.exp2(-8.0 * jnp.arange(1, n + 1, dtype=F32) / n)


def kernel(x, p, attn_norm, ffn_norm, a_w_qkv, a_q_norm, a_k_norm, a_sink, a_w_o, b_w_qkv, b_q_norm, b_k_norm, b_lambda, b_subln, b_w_o, w_ffn_in, w_ffn_out, ple_w_proj, ple_post_norm, ple_gate_norm, ple_w_gate):
    batch, seq, d = x.shape
    m = batch * seq
    h = x.reshape(m, d)
    p_all = p.reshape(DEPTH, m, PLE_DIM)
    w_qkv = a_w_qkv[0].astype(BF16)
    for i in range(DEPTH):
        j = i // 2
        if i % 2 == 0:
            col_gain = jnp.concatenate([jnp.tile(a_q_norm[j] * (A_DH ** -0.5 * LOG2E), A_HQ),
                                        jnp.tile(a_k_norm[j], A_KV), jnp.ones((A_NK,), F32)])
            col_mask = jnp.concatenate([jnp.ones((A_NQ + A_NK,), F32), jnp.zeros((A_NK,), F32)])
            qkv = _qkv_proj(h, attn_norm[i], w_qkv, col_gain, col_mask, A_DH,
                            QKV_A_TM, QKV_A_TN, QKV_A_SUB, A_QKV // QKV_A_TN)
            mix, (w_o, w_in, w_out, w_gate, w_proj, w_qkv) = _attn_a(
                qkv, _alibi_slopes(A_HQ) * LOG2E, a_sink[j].astype(F32) * LOG2E, batch, seq,
                [(a_w_o, j, False), (w_ffn_in, i, True), (w_ffn_out, i, False), (ple_w_gate, i, False),
                 (ple_w_proj, i, False), (b_w_qkv, j, False)])
        else:
            lambda_init = 0.8 - 0.6 * math.exp(-0.3 * i)
            col_gain = jnp.concatenate([jnp.tile(b_q_norm[j] * (B_DH ** -0.5 * LOG2E), 2 * B_H),
                                        jnp.tile(b_k_norm[j], 2 * B_H), jnp.ones((B_W,), F32)])
            col_mask = jnp.concatenate([jnp.ones((2 * B_W,), F32), jnp.zeros((B_W,), F32)])
            qkv = _qkv_proj(h, attn_norm[i], w_qkv, col_gain, col_mask, B_DH,
                            QKV_B_TM, QKV_B_TN, QKV_B_SUB, 2 * B_W // QKV_B_TN)
            mix, _ = _attn_b(qkv, _alibi_slopes(B_H) * LOG2E, b_lambda[j], b_subln[j], batch, seq, lambda_init, [])
            w_o, w_in, w_out, w_gate, w_proj = next_weights
        h = _proj_res(mix, w_o, h)
        later = [] if i + 1 == DEPTH else [
            (b_w_o, 0, False), (w_ffn_in, i + 1, True), (w_ffn_out, i + 1, False), (ple_w_gate, i + 1, False),
            (ple_w_proj, i + 1, False)]
        h, next_weights = _ffn(h, ffn_norm[i], w_in, w_out, later)
        h = _ple(h, p_all, i, ple_gate_norm[i], w_gate, w_proj, ple_post_norm[i])
    return h.reshape(batch, seq, d)
```

```python
import functools
import math

import jax
import jax.numpy as jnp
from jax import lax
from jax.experimental import pallas as pl
from jax.experimental.pallas import tpu as pltpu

D_MODEL = 2048
DEPTH = 2
BLK = 128
WIN = 128
A_DH = 64
A_HQ = D_MODEL // A_DH
A_KV = 4
A_G = A_HQ // A_KV
A_NQ = A_HQ * A_DH
A_NK = A_KV * A_DH
A_QKV = A_NQ + 2 * A_NK
B_DH = 128
B_H = D_MODEL // (2 * B_DH)
B_W = B_H * 2 * B_DH
B_QKV = 3 * B_W
D_FF = int(math.ceil(8 * D_MODEL / 3 / 256) * 256)
PLE_DIM = 256
EPS = 1e-6
MASKED_DIST = 1e30
LOG2E = math.log2(math.e)

V7X_VMEM_LIMIT = 60 * 1024 * 1024
LANES = 128
MXU_DIM = 256
BF16_TILE_ROWS = 16
QKV_ROW_SPLIT = 2
QKV_A_TM, QKV_A_TN, QKV_A_SUB = 512, A_QKV, 512
QKV_B_TM, QKV_B_TN, QKV_B_SUB = 1024, B_W, 512
PROJ_TM, PROJ_SUB = 512, 512
FFN_TM, FFN_TF = 1024, 512
PLE_TM, PLE_SUB, PLE_ROW_SPLIT = 1024, 512, 4
ATTN_A_QBLOCKS = 4
ATTN_B_TQ, ATTN_B_SUBS, ATTN_B_KCHUNK = 1024, 4, 2 * LANES

F32 = jnp.float32
BF16 = jnp.bfloat16


def _params(*sem):
    return pltpu.CompilerParams(dimension_semantics=sem, vmem_limit_bytes=V7X_VMEM_LIMIT)


def _row_rms(x, gain):
    return x * lax.rsqrt(jnp.mean(x * x, axis=-1, keepdims=True) + EPS) * gain


def _dot(a, b):
    return jnp.dot(a, b, preferred_element_type=F32)


def _dot_nt(a, b):
    return lax.dot_general(a, b, (((1,), (1,)), ((), ())), preferred_element_type=F32)


def _cast_plan(weights, n_steps):
    plan = []
    for w_all, layer, chunked in weights:
        tiles = w_all.shape[1] // BF16_TILE_ROWS
        nblk = max(k for k in range(1, min(tiles, n_steps) + 1) if tiles % k == 0)
        plan.append((w_all, layer, nblk, w_all.shape[1] // nblk, chunked))
    return plan


def _cast_specs(plan, step_of):
    in_specs, out_specs, out_shapes = [], [], []
    for w_all, layer, nblk, rows, chunked in plan:
        _, total_rows, cols = w_all.shape
        in_specs.append(pl.BlockSpec(
            (None, rows, cols),
            lambda *ids, layer=layer, nblk=nblk: (layer, jnp.minimum(step_of(*ids), nblk - 1), 0)))
        if chunked:
            nf = cols // (2 * FFN_TF)
            out_specs.append(pl.BlockSpec(
                (nf, rows, 2 * FFN_TF), lambda *ids, nblk=nblk: (0, jnp.minimum(step_of(*ids), nblk - 1), 0)))
            out_shapes.append(jax.ShapeDtypeStruct((nf, total_rows, 2 * FFN_TF), BF16))
        else:
            out_specs.append(pl.BlockSpec(
                (rows, cols), lambda *ids, nblk=nblk: (jnp.minimum(step_of(*ids), nblk - 1), 0)))
            out_shapes.append(jax.ShapeDtypeStruct((total_rows, cols), BF16))
    return in_specs, out_specs, out_shapes


def _cast_blocks(w_refs, o_refs):
    for w_ref, o_ref in zip(w_refs, o_refs):
        if len(o_ref.shape) == 3:
            nf = o_ref.shape[0]
            for f in range(nf):
                o_ref[f, :, :FFN_TF] = w_ref[:, f * FFN_TF:(f + 1) * FFN_TF].astype(BF16)
                o_ref[f, :, FFN_TF:] = w_ref[:, (nf + f) * FFN_TF:(nf + f + 1) * FFN_TF].astype(BF16)
        else:
            o_ref[...] = w_ref[...].astype(BF16)


def _head_sumsq(tile, group):
    sq = tile * tile
    if group == LANES:
        return jnp.sum(sq, axis=-1, keepdims=True)
    assert 2 * group == LANES
    low = lax.broadcasted_iota(jnp.int32, tile.shape, 1) < group
    low_sum = jnp.sum(jnp.where(low, sq, 0.0), axis=-1, keepdims=True)
    high_sum = jnp.sum(jnp.where(low, 0.0, sq), axis=-1, keepdims=True)
    return jnp.where(low, low_sum, high_sum)


def _qkv_kernel(x_ref, g_ref, w_ref, cg_ref, cm_ref, o_ref, hn_ref, *, group, n_norm, n_steps, sub):
    j = pl.program_id(1)
    tn = o_ref.shape[1]

    def tiles(normed, rows=slice(None)):
        for c in range(tn // sub):
            cols = slice(c * sub, (c + 1) * sub)
            o = _dot(hn_ref[rows, :], w_ref[:, cols])
            if normed:
                parts = []
                for t in range(sub // LANES):
                    tile = o[:, t * LANES:(t + 1) * LANES]
                    parts.append(tile * lax.rsqrt(_head_sumsq(tile, group) * (1.0 / group) + EPS))
                nrm = jnp.concatenate(parts, axis=1) * cg_ref[:, cols]
                o = jnp.where(cm_ref[:, cols] > 0.0, nrm, o)
            o_ref[rows, cols] = o.astype(o_ref.dtype)

    @pl.when(j == 0)
    def _():
        piece = o_ref.shape[0] // QKV_ROW_SPLIT
        for u in range(QKV_ROW_SPLIT):
            rows = slice(u * piece, (u + 1) * piece)
            hn_ref[rows, :] = _row_rms(x_ref[rows, :], g_ref[...]).astype(BF16)
            tiles(n_norm > 0, rows)

    if n_norm > 1:
        pl.when((j > 0) & (j < n_norm))(lambda: tiles(True))
    if n_steps > max(n_norm, 1):
        pl.when(j >= max(n_norm, 1))(lambda: tiles(False))


def _qkv_proj(x, gain, w, col_gain, col_mask, group, tm, tn, sub, n_norm):
    m, d = x.shape
    n = w.shape[1]
    return pl.pallas_call(
        functools.partial(_qkv_kernel, group=group, n_norm=n_norm, n_steps=n // tn, sub=sub),
        grid=(m // tm, n // tn),
        in_specs=[
            pl.BlockSpec((tm, d), lambda i, j: (i, 0)),
            pl.BlockSpec((1, d), lambda i, j: (0, 0)),
            pl.BlockSpec((d, tn), lambda i, j: (0, j)),
            pl.BlockSpec((1, tn), lambda i, j: (0, j)),
            pl.BlockSpec((1, tn), lambda i, j: (0, j)),
        ],
        out_specs=pl.BlockSpec((tm, tn), lambda i, j: (i, j)),
        out_shape=jax.ShapeDtypeStruct((m, n), BF16),
        scratch_shapes=[pltpu.VMEM((tm, d), BF16)],
        compiler_params=_params("parallel", "arbitrary"),
        name="qkv_proj",
    )(x, gain.reshape(1, d), w, col_gain.reshape(1, n), col_mask.reshape(1, n))


def _attn_a_kernel(slope_ref, sink_ref, hm_ref, q_ref, kvp_ref, kvc_ref, kvn_ref, *rest, seq, n_cast):
    w_refs, o_ref, wo_refs = rest[:n_cast], rest[n_cast], rest[n_cast + 1:]
    _cast_blocks(w_refs, wo_refs)
    kv_all = jnp.concatenate([kvp_ref[...], kvc_ref[...], kvn_ref[...]], axis=0)
    for u in range(ATTN_A_QBLOCKS):
        _attn_a_block(slope_ref, sink_ref, hm_ref, q_ref, o_ref, kv_all[u * BLK:u * BLK + BLK + 2 * WIN],
                      slice(u * BLK, (u + 1) * BLK), pl.program_id(1) * ATTN_A_QBLOCKS + u, seq)


def _attn_a_block(slope_ref, sink_ref, hm_ref, q_ref, o_ref, kv, rows, n, seq):
    span = BLK + 2 * WIN
    pairs = A_G // 2
    ti = lax.broadcasted_iota(jnp.int32, (BLK, span), 0)
    sj = lax.broadcasted_iota(jnp.int32, (BLK, span), 1)
    dist_i = jnp.abs(ti + WIN - sj)
    pos = sj + (n * BLK - WIN)
    valid = (dist_i <= WIN) & (pos >= 0) & (pos < seq)
    dist = jnp.where(valid, dist_i.astype(F32), MASKED_DIST)
    low_half = lax.broadcasted_iota(jnp.int32, (BLK, LANES), 1) < A_DH
    keep_lo = hm_ref[0:1, :]
    keep_hi = hm_ref[1:2, :]

    def block_diag(tile, head_in_low_half):
        swapped = jnp.concatenate([tile[:, A_DH:], tile[:, :A_DH]], axis=1)
        lo, hi = (tile, swapped) if head_in_low_half else (swapped, tile)
        return jnp.concatenate([lo * keep_lo, hi * keep_hi], axis=0)

    for kvh in range(A_KV):
        t0 = (kvh // 2) * LANES
        kbd = block_diag(kv[:, t0:t0 + LANES], kvh % 2 == 0)
        vbd = block_diag(kv[:, A_NK + t0:A_NK + t0 + LANES], kvh % 2 == 0)
        q0 = kvh * pairs * LANES
        qp = jnp.concatenate([q_ref[rows, q0 + pp * LANES:q0 + (pp + 1) * LANES] for pp in range(pairs)], axis=0)
        s = _dot_nt(qp, kbd)
        probs, scales = [], []
        for pp in range(pairs):
            halves, rls = [], []
            for e in range(2):
                h = kvh * A_G + 2 * pp + e
                sh = s[pp * BLK:(pp + 1) * BLK, e * span:(e + 1) * span] - slope_ref[h] * dist
                sink = sink_ref[h]
                mx = jnp.maximum(jnp.max(sh, axis=-1, keepdims=True), sink)
                p = jnp.exp2(sh - mx)
                l = jnp.sum(p, axis=-1, keepdims=True) + jnp.exp2(sink - mx)
                halves.append(p.astype(BF16))
                rls.append(1.0 / l)
            probs.append(jnp.concatenate(halves, axis=1))
            scales.append(jnp.where(low_half, rls[0], rls[1]))
        pv = _dot(jnp.concatenate(probs, axis=0), vbd)
        for pp in range(pairs):
            o_ref[rows, q0 + pp * LANES:q0 + (pp + 1) * LANES] = (
                pv[pp * BLK:(pp + 1) * BLK] * scales[pp]).astype(o_ref.dtype)


def _attn_a(qkv, slopes, sink, batch, seq, cast_weights):
    nb = seq // BLK
    qb = ATTN_A_QBLOCKS
    ns = nb // qb
    kv_col = A_NQ // (2 * A_NK)
    lane = jnp.arange(LANES)
    half_masks = jnp.stack([lane < A_DH, lane >= A_DH]).astype(BF16)
    plan = _cast_plan(cast_weights, batch * ns)
    cast_in, cast_out, cast_shapes = _cast_specs(plan, lambda b, n: b * ns + n)
    outs = pl.pallas_call(
        functools.partial(_attn_a_kernel, seq=seq, n_cast=len(plan)),
        grid=(batch, ns),
        in_specs=[
            pl.BlockSpec(memory_space=pltpu.SMEM),
            pl.BlockSpec(memory_space=pltpu.SMEM),
            pl.BlockSpec((2, LANES), lambda b, n: (0, 0)),
            pl.BlockSpec((qb * BLK, A_NQ), lambda b, n: (b * ns + n, 0)),
            pl.BlockSpec((BLK, 2 * A_NK), lambda b, n: (b * nb + jnp.maximum(n * qb - 1, 0), kv_col)),
            pl.BlockSpec((qb * BLK, 2 * A_NK), lambda b, n: (b * ns + n, kv_col)),
            pl.BlockSpec((BLK, 2 * A_NK), lambda b, n: (b * nb + jnp.minimum(n * qb + qb, nb - 1), kv_col)),
        ] + cast_in,
        out_specs=[pl.BlockSpec((qb * BLK, A_NQ), lambda b, n: (b * ns + n, 0))] + cast_out,
        out_shape=[jax.ShapeDtypeStruct((batch * seq, A_NQ), BF16)] + cast_shapes,
        compiler_params=_params("arbitrary", "arbitrary"),
        name="attn_a",
    )(slopes, sink, half_masks, qkv, qkv, qkv, qkv, *[p[0] for p in plan])
    return outs[0], outs[1:]


def _attn_b_block(t, batch, nq):
    t = jnp.clip(t, 0, B_H * batch * nq - 1)
    return t // (nq * batch), (t // nq) % batch, t % nq


def _attn_b_kernel(slope_ref, lam_ref, sub_ref, q_ref, k_ref, v_ref, *rest, lambda_init, batch, nq, n_cast):
    w_refs, o_ref, wo_refs = rest[:n_cast], rest[n_cast], rest[n_cast + 1:2 * n_cast + 1]
    bias_ref, s_ref, m_ref = rest[2 * n_cast + 1:]
    t = pl.program_id(0)
    tq = q_ref.shape[0]
    seq = k_ref.shape[0]
    ts = tq // ATTN_B_SUBS
    h, b, i = _attn_b_block(t, batch, nq)
    row0 = pl.multiple_of(i * tq, tq)

    @pl.when(t == 0)
    def _():
        s_ref[...] = jnp.zeros_like(s_ref)
        m_ref[...] = jnp.zeros_like(m_ref)

    @pl.when(b == 0)
    def _():
        tpos = (lax.broadcasted_iota(jnp.int32, (tq, seq), 0) + i * tq).astype(F32)
        spos = lax.broadcasted_iota(jnp.int32, (tq, seq), 1).astype(F32)
        bias_ref[pl.ds(row0, tq), :] = slope_ref[h] * jnp.abs(tpos - spos)

    lv = lam_ref[...]
    lam = (jnp.exp(jnp.sum(lv[0:1] * lv[1:2], axis=-1, keepdims=True))
           - jnp.exp(jnp.sum(lv[2:3] * lv[3:4], axis=-1, keepdims=True)) + lambda_init)

    _cast_blocks(w_refs, wo_refs)
    for u in range(ATTN_B_SUBS):
        sub_rows = slice(u * ts, (u + 1) * ts)
        bias_rows = pl.ds(row0 + u * ts, ts)
        pv = [None, None]
        l_part = [None, None]
        m_run = [None, None]
        for n in range(seq // ATTN_B_KCHUNK):
            k0 = n * ATTN_B_KCHUNK
            keys = slice(k0, k0 + ATTN_B_KCHUNK)
            for c in range(2):
                m_rep = m_ref[u, c]
                ps = [jnp.exp2(s_ref[u, c, :, k0 + j * LANES:k0 + (j + 1) * LANES] - m_rep)
                      for j in range(ATTN_B_KCHUNK // LANES)]
                l_new = functools.reduce(jnp.add, ps)
                l_part[c] = l_new if n == 0 else l_part[c] + l_new
                pv_new = _dot(jnp.concatenate(ps, axis=1).astype(BF16), v_ref[keys, :])
                pv[c] = pv_new if n == 0 else pv[c] + pv_new
            for c in range(2):
                dims = slice(c * B_DH, (c + 1) * B_DH)
                s = _dot_nt(q_ref[sub_rows, dims], k_ref[keys, dims]) - bias_ref[bias_rows, keys]
                s_ref[u, c, :, keys] = s
                part = functools.reduce(
                    jnp.maximum, [s[:, j * LANES:(j + 1) * LANES] for j in range(ATTN_B_KCHUNK // LANES)])
                m_run[c] = part if n == 0 else jnp.maximum(m_run[c], part)
        l1 = jnp.sum(l_part[0], axis=-1, keepdims=True)
        l2 = jnp.sum(l_part[1], axis=-1, keepdims=True)
        o = pv[0] * (1.0 / l1) - pv[1] * (lam / l2)
        o_ref[sub_rows, :] = (_row_rms(o, sub_ref[...]) * (1.0 - lambda_init)).astype(o_ref.dtype)
        for c in range(2):
            m_ref[u, c] = jnp.broadcast_to(jnp.max(m_run[c], axis=-1, keepdims=True), (ts, LANES))


def _attn_b(qkv, slopes, lam_vecs, subln, batch, seq, lambda_init, cast_weights):
    tq = ATTN_B_TQ
    ts = tq // ATTN_B_SUBS
    nq = seq // tq
    e = 2 * B_DH
    n_steps = B_H * batch * nq + 1

    def q_map(t):
        h, b, i = _attn_b_block(t, batch, nq)
        return b * nq + i, h

    def k_map(t):
        h, b, _ = _attn_b_block(t, batch, nq)
        return b, B_H + h

    def v_map(t):
        h, b, _ = _attn_b_block(t - 1, batch, nq)
        return b, 2 * B_H + h

    plan = _cast_plan(cast_weights, n_steps)
    cast_in, cast_out, cast_shapes = _cast_specs(plan, lambda t: t)
    outs = pl.pallas_call(
        functools.partial(_attn_b_kernel, lambda_init=lambda_init, batch=batch, nq=nq, n_cast=len(plan)),
        grid=(n_steps,),
        in_specs=[
            pl.BlockSpec(memory_space=pltpu.SMEM),
            pl.BlockSpec((4, B_DH), lambda t: (0, 0)),
            pl.BlockSpec((1, e), lambda t: (0, 0)),
            pl.BlockSpec((tq, e), q_map),
            pl.BlockSpec((seq, e), k_map),
            pl.BlockSpec((seq, e), v_map),
        ] + cast_in,
        out_specs=[pl.BlockSpec((tq, e), lambda t: q_map(t - 1))] + cast_out,
        out_shape=[jax.ShapeDtypeStruct((batch * seq, B_W), BF16)] + cast_shapes,
        scratch_shapes=[pltpu.VMEM((seq, seq), F32),
                        pltpu.VMEM((ATTN_B_SUBS, 2, ts, seq), F32),
                        pltpu.VMEM((ATTN_B_SUBS, 2, ts, LANES), F32)],
        compiler_params=_params("arbitrary"),
        name="attn_b",
    )(slopes, lam_vecs, subln.reshape(1, e), qkv, qkv, qkv, *[p[0] for p in plan])
    return outs[0], outs[1:]


def _proj_res_kernel(a_ref, w_ref, r_ref, o_ref):
    for c in range(o_ref.shape[1] // PROJ_SUB):
        cols = slice(c * PROJ_SUB, (c + 1) * PROJ_SUB)
        o_ref[:, cols] = r_ref[:, cols] + _dot(a_ref[...], w_ref[:, cols])


def _proj_res(a, w, res):
    m, kdim = a.shape
    n = w.shape[1]
    tm = PROJ_TM
    return pl.pallas_call(
        _proj_res_kernel,
        grid=(m // tm,),
        in_specs=[
            pl.BlockSpec((tm, kdim), lambda i: (i, 0)),
            pl.BlockSpec((kdim, n), lambda i: (0, 0)),
            pl.BlockSpec((tm, n), lambda i: (i, 0)),
        ],
        out_specs=pl.BlockSpec((tm, n), lambda i: (i, 0)),
        out_shape=jax.ShapeDtypeStruct((m, n), F32),
        compiler_params=_params("parallel"),
        name="proj_res",
    )(a, w, res)


def _ffn_kernel(x_ref, g_ref, wgu_ref, wo_ref, o_ref, hn_ref):
    f = pl.program_id(1)
    tf = wo_ref.shape[0]

    def chunk(base_ref):
        gu = _dot(hn_ref[...], wgu_ref[...])
        gate = gu[:, :tf]
        act = (gate * jax.nn.sigmoid(gate) * gu[:, tf:]).astype(BF16)
        o_ref[...] = base_ref[...] + _dot(act, wo_ref[...])

    @pl.when(f == 0)
    def _():
        hn_ref[...] = _row_rms(x_ref[...], g_ref[...]).astype(BF16)
        chunk(x_ref)

    @pl.when(f > 0)
    def _():
        chunk(o_ref)


def _ffn(x, gain, w_in_chunks, w_out):
    m, d = x.shape
    tm, tf = FFN_TM, FFN_TF
    nf = D_FF // tf
    return pl.pallas_call(
        _ffn_kernel,
        grid=(m // tm, nf),
        in_specs=[
            pl.BlockSpec((tm, d), lambda i, f: (i, 0)),
            pl.BlockSpec((1, d), lambda i, f: (0, 0)),
            pl.BlockSpec((None, d, 2 * tf), lambda i, f: (f, 0, 0)),
            pl.BlockSpec((tf, d), lambda i, f: (f, 0)),
        ],
        out_specs=pl.BlockSpec((tm, d), lambda i, f: (i, 0)),
        out_shape=jax.ShapeDtypeStruct((m, d), F32),
        scratch_shapes=[pltpu.VMEM((tm, d), BF16)],
        compiler_params=_params("parallel", "arbitrary"),
        name="ffn",
    )(x, gain.reshape(1, d), w_in_chunks, w_out)


def _ple_kernel(h_ref, p_ref, gn_ref, wg_ref, wp_ref, pn_ref, o_ref):
    for u in range(PLE_ROW_SPLIT):
        rows = slice(u * (o_ref.shape[0] // PLE_ROW_SPLIT), (u + 1) * (o_ref.shape[0] // PLE_ROW_SPLIT))
        hn = _row_rms(h_ref[rows, :], gn_ref[...]).astype(BF16)
        pe = _row_rms(_dot(p_ref[rows, :].astype(BF16), wp_ref[...]), pn_ref[...])
        for c in range(o_ref.shape[1] // PLE_SUB):
            cols = slice(c * PLE_SUB, (c + 1) * PLE_SUB)
            gate = jax.nn.sigmoid(_dot(hn, wg_ref[:, cols]))
            o_ref[rows, cols] = h_ref[rows, cols] + pe[:, cols] * gate


def _ple(h, p_all, layer, gate_norm, w_gate, w_proj, post_norm):
    m, d = h.shape
    tm = PLE_TM
    return pl.pallas_call(
        _ple_kernel,
        grid=(m // tm,),
        in_specs=[
            pl.BlockSpec((tm, d), lambda i: (i, 0)),
            pl.BlockSpec((None, tm, PLE_DIM), lambda i: (layer, i, 0)),
            pl.BlockSpec((1, d), lambda i: (0, 0)),
            pl.BlockSpec((d, d), lambda i: (0, 0)),
            pl.BlockSpec((PLE_DIM, d), lambda i: (0, 0)),
            pl.BlockSpec((1, d), lambda i: (0, 0)),
        ],
        out_specs=pl.BlockSpec((tm, d), lambda i: (i, 0)),
        out_shape=jax.ShapeDtypeStruct((m, d), F32),
        compiler_params=_params("parallel"),
        name="ple",
    )(h, p_all, gate_norm.reshape(1, d), w_gate, w_proj, post_norm.reshape(1, d))


def _alibi_slopes(n):
    return jnp.exp2(-8.0 * jnp.arange(1, n + 1, dtype=F32) / n)


def kernel(x, p, attn_norm, ffn_norm, a_w_qkv, a_q_norm, a_k_norm, a_sink, a_w_o, b_w_qkv, b_q_norm, b_k_norm, b_lambda, b_subln, b_w_o, w_ffn_in, w_ffn_out, ple_w_proj, ple_post_norm, ple_gate_norm, ple_w_gate):
    batch, seq, d = x.shape
    m = batch * seq
    h = x.reshape(m, d)
    p_all = p.reshape(DEPTH, m, PLE_DIM)
    w_qkv = a_w_qkv[0].astype(BF16)
    for i in range(DEPTH):
        j = i // 2
        if i % 2 == 0:
            col_gain = jnp.concatenate([jnp.tile(a_q_norm[j] * (A_DH ** -0.5 * LOG2E), A_HQ),
                                        jnp.tile(a_k_norm[j], A_KV), jnp.ones((A_NK,), F32)])
            col_mask = jnp.concatenate([jnp.ones((A_NQ + A_NK,), F32), jnp.zeros((A_NK,), F32)])
            qkv = _qkv_proj(h, attn_norm[i], w_qkv, col_gain, col_mask, A_DH,
                            QKV_A_TM, QKV_A_TN, QKV_A_SUB, A_QKV // QKV_A_TN)
            mix, (w_o, w_in, w_out, w_gate, w_proj, w_qkv) = _attn_a(
                qkv, _alibi_slopes(A_HQ) * LOG2E, a_sink[j].astype(F32) * LOG2E, batch, seq,
                [(a_w_o, j, False), (w_ffn_in, i, True), (w_ffn_out, i, False), (ple_w_gate, i, False),
                 (ple_w_proj, i, False), (b_w_qkv, j, False)])
        else:
            lambda_init = 0.8 - 0.6 * math.exp(-0.3 * i)
            col_gain = jnp.concatenate([jnp.tile(b_q_norm[j] * (B_DH ** -0.5 * LOG2E), 2 * B_H),
                                        jnp.tile(b_k_norm[j], 2 * B_H), jnp.ones((B_W,), F32)])
            col_mask = jnp.concatenate([jnp.ones((2 * B_W,), F32), jnp.zeros((B_W,), F32)])
            qkv = _qkv_proj(h, attn_norm[i], w_qkv, col_gain, col_mask, B_DH,
                            QKV_B_TM, QKV_B_TN, QKV_B_SUB, 2 * B_W // QKV_B_TN)
            mix, (w_o, w_in, w_out, w_gate, w_proj) = _attn_b(
                qkv, _alibi_slopes(B_H) * LOG2E, b_lambda[j], b_subln[j], batch, seq, lambda_init,
                [(b_w_o, j, False), (w_ffn_in, i, True), (w_ffn_out, i, False), (ple_w_gate, i, False),
                 (ple_w_proj, i, False)])
        h = _proj_res(mix, w_o, h)
        h = _ffn(h, ffn_norm[i], w_in, w_out)
        h = _ple(h, p_all, i, ple_gate_norm[i], w_gate, w_proj, ple_post_norm[i])
    return h.reshape(batch, seq, d)
```

```python
import functools
import math

import jax
import jax.numpy as jnp
from jax import lax
from jax.experimental import pallas as pl
from jax.experimental.pallas import tpu as pltpu

D_MODEL = 2048
DEPTH = 2
BLK = 128
WIN = 128
A_DH = 64
A_HQ = D_MODEL // A_DH
A_KV = 4
A_G = A_HQ // A_KV
A_NQ = A_HQ * A_DH
A_NK = A_KV * A_DH
A_QKV = A_NQ + 2 * A_NK
B_DH = 128
B_H = D_MODEL // (2 * B_DH)
B_W = B_H * 2 * B_DH
B_QKV = 3 * B_W
D_FF = int(math.ceil(8 * D_MODEL / 3 / 256) * 256)
PLE_DIM = 256
EPS = 1e-6
MASKED_DIST = 1e30
LOG2E = math.log2(math.e)

V7X_VMEM_LIMIT = 60 * 1024 * 1024
LANES = 128
MXU_DIM = 256
BF16_TILE_ROWS = 16
QKV_ROW_SPLIT = 2
QKV_A_TM, QKV_A_TN, QKV_A_SUB = 512, A_QKV, 512
QKV_B_TM, QKV_B_TN, QKV_B_SUB = 1024, B_W, 512
PROJ_TM, PROJ_SUB = 1024, 512
FFN_TM, FFN_TF = 1024, 512
PLE_TM, PLE_SUB, PLE_ROW_SPLIT = 1024, 512, 4
ATTN_A_QBLOCKS = 4
ATTN_B_TQ, ATTN_B_SUBS, ATTN_B_KCHUNK = 1024, 4, 2 * LANES

F32 = jnp.float32
BF16 = jnp.bfloat16


def _params(*sem):
    return pltpu.CompilerParams(dimension_semantics=sem, vmem_limit_bytes=V7X_VMEM_LIMIT)


def _row_rms(x, gain):
    return x * lax.rsqrt(jnp.mean(x * x, axis=-1, keepdims=True) + EPS) * gain


def _dot(a, b):
    return jnp.dot(a, b, preferred_element_type=F32)


def _dot_nt(a, b):
    return lax.dot_general(a, b, (((1,), (1,)), ((), ())), preferred_element_type=F32)


def _cast_plan(weights, n_steps):
    plan = []
    for w_all, layer, chunked in weights:
        tiles = w_all.shape[1] // BF16_TILE_ROWS
        nblk = max(k for k in range(1, min(tiles, n_steps) + 1) if tiles % k == 0)
        plan.append((w_all, layer, nblk, w_all.shape[1] // nblk, chunked))
    return plan


def _cast_specs(plan, step_of):
    in_specs, out_specs, out_shapes = [], [], []
    for w_all, layer, nblk, rows, chunked in plan:
        _, total_rows, cols = w_all.shape
        in_specs.append(pl.BlockSpec(
            (None, rows, cols),
            lambda *ids, layer=layer, nblk=nblk: (layer, jnp.minimum(step_of(*ids), nblk - 1), 0)))
        if chunked:
            nf = cols // (2 * FFN_TF)
            out_specs.append(pl.BlockSpec(
                (nf, rows, 2 * FFN_TF), lambda *ids, nblk=nblk: (0, jnp.minimum(step_of(*ids), nblk - 1), 0)))
            out_shapes.append(jax.ShapeDtypeStruct((nf, total_rows, 2 * FFN_TF), BF16))
        else:
            out_specs.append(pl.BlockSpec(
                (rows, cols), lambda *ids, nblk=nblk: (jnp.minimum(step_of(*ids), nblk - 1), 0)))
            out_shapes.append(jax.ShapeDtypeStruct((total_rows, cols), BF16))
    return in_specs, out_specs, out_shapes


def _cast_blocks(w_refs, o_refs):
    for w_ref, o_ref in zip(w_refs, o_refs):
        if len(o_ref.shape) == 3:
            nf = o_ref.shape[0]
            for f in range(nf):
                o_ref[f, :, :FFN_TF] = w_ref[:, f * FFN_TF:(f + 1) * FFN_TF].astype(BF16)
                o_ref[f, :, FFN_TF:] = w_ref[:, (nf + f) * FFN_TF:(nf + f + 1) * FFN_TF].astype(BF16)
        else:
            o_ref[...] = w_ref[...].astype(BF16)


def _head_sumsq(tile, group):
    sq = tile * tile
    if group == LANES:
        return jnp.sum(sq, axis=-1, keepdims=True)
    assert 2 * group == LANES
    low = lax.broadcasted_iota(jnp.int32, tile.shape, 1) < group
    low_sum = jnp.sum(jnp.where(low, sq, 0.0), axis=-1, keepdims=True)
    high_sum = jnp.sum(jnp.where(low, 0.0, sq), axis=-1, keepdims=True)
    return jnp.where(low, low_sum, high_sum)


def _qkv_kernel(x_ref, g_ref, w_ref, cg_ref, cm_ref, o_ref, hn_ref, *, group, n_norm, n_steps, sub):
    j = pl.program_id(1)
    tn = o_ref.shape[1]

    def tiles(normed, rows=slice(None)):
        for c in range(tn // sub):
            cols = slice(c * sub, (c + 1) * sub)
            o = _dot(hn_ref[rows, :], w_ref[:, cols])
            if normed:
                parts = []
                for t in range(sub // LANES):
                    tile = o[:, t * LANES:(t + 1) * LANES]
                    parts.append(tile * lax.rsqrt(_head_sumsq(tile, group) * (1.0 / group) + EPS))
                nrm = jnp.concatenate(parts, axis=1) * cg_ref[:, cols]
                o = jnp.where(cm_ref[:, cols] > 0.0, nrm, o)
            o_ref[rows, cols] = o.astype(o_ref.dtype)

    @pl.when(j == 0)
    def _():
        piece = o_ref.shape[0] // QKV_ROW_SPLIT
        for u in range(QKV_ROW_SPLIT):
            rows = slice(u * piece, (u + 1) * piece)
            hn_ref[rows, :] = _row_rms(x_ref[rows, :], g_ref[...]).astype(BF16)
            tiles(n_norm > 0, rows)

    if n_norm > 1:
        pl.when((j > 0) & (j < n_norm))(lambda: tiles(True))
    if n_steps > max(n_norm, 1):
        pl.when(j >= max(n_norm, 1))(lambda: tiles(False))


def _qkv_proj(x, gain, w, col_gain, col_mask, group, tm, tn, sub, n_norm):
    m, d = x.shape
    n = w.shape[1]
    return pl.pallas_call(
        functools.partial(_qkv_kernel, group=group, n_norm=n_norm, n_steps=n // tn, sub=sub),
        grid=(m // tm, n // tn),
        in_specs=[
            pl.BlockSpec((tm, d), lambda i, j: (i, 0)),
            pl.BlockSpec((1, d), lambda i, j: (0, 0)),
            pl.BlockSpec((d, tn), lambda i, j: (0, j)),
            pl.BlockSpec((1, tn), lambda i, j: (0, j)),
            pl.BlockSpec((1, tn), lambda i, j: (0, j)),
        ],
        out_specs=pl.BlockSpec((tm, tn), lambda i, j: (i, j)),
        out_shape=jax.ShapeDtypeStruct((m, n), BF16),
        scratch_shapes=[pltpu.VMEM((tm, d), BF16)],
        compiler_params=_params("parallel", "arbitrary"),
        name="qkv_proj",
    )(x, gain.reshape(1, d), w, col_gain.reshape(1, n), col_mask.reshape(1, n))


def _attn_a_kernel(slope_ref, sink_ref, hm_ref, q_ref, kvp_ref, kvc_ref, kvn_ref, *rest, seq, n_cast):
    w_refs, o_ref, wo_refs = rest[:n_cast], rest[n_cast], rest[n_cast + 1:]
    _cast_blocks(w_refs, wo_refs)
    kv_all = jnp.concatenate([kvp_ref[...], kvc_ref[...], kvn_ref[...]], axis=0)
    for u in range(ATTN_A_QBLOCKS):
        _attn_a_block(slope_ref, sink_ref, hm_ref, q_ref, o_ref, kv_all[u * BLK:u * BLK + BLK + 2 * WIN],
                      slice(u * BLK, (u + 1) * BLK), pl.program_id(1) * ATTN_A_QBLOCKS + u, seq)


def _attn_a_block(slope_ref, sink_ref, hm_ref, q_ref, o_ref, kv, rows, n, seq):
    span = BLK + 2 * WIN
    pairs = A_G // 2
    ti = lax.broadcasted_iota(jnp.int32, (BLK, span), 0)
    sj = lax.broadcasted_iota(jnp.int32, (BLK, span), 1)
    dist_i = jnp.abs(ti + WIN - sj)
    pos = sj + (n * BLK - WIN)
    valid = (dist_i <= WIN) & (pos >= 0) & (pos < seq)
    dist = jnp.where(valid, dist_i.astype(F32), MASKED_DIST)
    low_half = lax.broadcasted_iota(jnp.int32, (BLK, LANES), 1) < A_DH
    keep_lo = hm_ref[0:1, :]
    keep_hi = hm_ref[1:2, :]

    def block_diag(tile, head_in_low_half):
        swapped = jnp.concatenate([tile[:, A_DH:], tile[:, :A_DH]], axis=1)
        lo, hi = (tile, swapped) if head_in_low_half else (swapped, tile)
        return jnp.concatenate([lo * keep_lo, hi * keep_hi], axis=0)

    for kvh in range(A_KV):
        t0 = (kvh // 2) * LANES
        kbd = block_diag(kv[:, t0:t0 + LANES], kvh % 2 == 0)
        vbd = block_diag(kv[:, A_NK + t0:A_NK + t0 + LANES], kvh % 2 == 0)
        q0 = kvh * pairs * LANES
        qp = jnp.concatenate([q_ref[rows, q0 + pp * LANES:q0 + (pp + 1) * LANES] for pp in range(pairs)], axis=0)
        s = _dot_nt(qp, kbd)
        probs, scales = [], []
        for pp in range(pairs):
            halves, rls = [], []
            for e in range(2):
                h = kvh * A_G + 2 * pp + e
                sh = s[pp * BLK:(pp + 1) * BLK, e * span:(e + 1) * span] - slope_ref[h] * dist
                sink = sink_ref[h]
                mx = jnp.maximum(jnp.max(sh, axis=-1, keepdims=True), sink)
                p = jnp.exp2(sh - mx)
                l = jnp.sum(p, axis=-1, keepdims=True) + jnp.exp2(sink - mx)
                halves.append(p.astype(BF16))
                rls.append(1.0 / l)
            probs.append(jnp.concatenate(halves, axis=1))
            scales.append(jnp.where(low_half, rls[0], rls[1]))
        pv = _dot(jnp.concatenate(probs, axis=0), vbd)
        for pp in range(pairs):
            o_ref[rows, q0 + pp * LANES:q0 + (pp + 1) * LANES] = (
                pv[pp * BLK:(pp + 1) * BLK] * scales[pp]).astype(o_ref.dtype)


def _attn_a(qkv, slopes, sink, batch, seq, cast_weights):
    nb = seq // BLK
    qb = ATTN_A_QBLOCKS
    ns = nb // qb
    kv_col = A_NQ // (2 * A_NK)
    lane = jnp.arange(LANES)
    half_masks = jnp.stack([lane < A_DH, lane >= A_DH]).astype(BF16)
    plan = _cast_plan(cast_weights, batch * ns)
    cast_in, cast_out, cast_shapes = _cast_specs(plan, lambda b, n: b * ns + n)
    outs = pl.pallas_call(
        functools.partial(_attn_a_kernel, seq=seq, n_cast=len(plan)),
        grid=(batch, ns),
        in_specs=[
            pl.BlockSpec(memory_space=pltpu.SMEM),
            pl.BlockSpec(memory_space=pltpu.SMEM),
            pl.BlockSpec((2, LANES), lambda b, n: (0, 0)),
            pl.BlockSpec((qb * BLK, A_NQ), lambda b, n: (b * ns + n, 0)),
            pl.BlockSpec((BLK, 2 * A_NK), lambda b, n: (b * nb + jnp.maximum(n * qb - 1, 0), kv_col)),
            pl.BlockSpec((qb * BLK, 2 * A_NK), lambda b, n: (b * ns + n, kv_col)),
            pl.BlockSpec((BLK, 2 * A_NK), lambda b, n: (b * nb + jnp.minimum(n * qb + qb, nb - 1), kv_col)),
        ] + cast_in,
        out_specs=[pl.BlockSpec((qb * BLK, A_NQ), lambda b, n: (b * ns + n, 0))] + cast_out,
        out_shape=[jax.ShapeDtypeStruct((batch * seq, A_NQ), BF16)] + cast_shapes,
        compiler_params=_params("arbitrary", "arbitrary"),
        name="attn_a",
    )(slopes, sink, half_masks, qkv, qkv, qkv, qkv, *[p[0] for p in plan])
    return outs[0], outs[1:]


def _attn_b_block(t, batch, nq):
    t = jnp.clip(t, 0, B_H * batch * nq - 1)
    return t // (nq * batch), (t // nq) % batch, t % nq


def _attn_b_kernel(slope_ref, lam_ref, sub_ref, q_ref, k_ref, v_ref, *rest, lambda_init, batch, nq, n_cast):
    w_refs, o_ref, wo_refs = rest[:n_cast], rest[n_cast], rest[n_cast + 1:2 * n_cast + 1]
    bias_ref, s_ref, m_ref = rest[2 * n_cast + 1:]
    t = pl.program_id(0)
    tq = q_ref.shape[0]
    seq = k_ref.shape[0]
    ts = tq // ATTN_B_SUBS
    h, b, i = _attn_b_block(t, batch, nq)
    row0 = pl.multiple_of(i * tq, tq)

    @pl.when(t == 0)
    def _():
        s_ref[...] = jnp.zeros_like(s_ref)
        m_ref[...] = jnp.zeros_like(m_ref)

    @pl.when(b == 0)
    def _():
        tpos = (lax.broadcasted_iota(jnp.int32, (tq, seq), 0) + i * tq).astype(F32)
        spos = lax.broadcasted_iota(jnp.int32, (tq, seq), 1).astype(F32)
        bias_ref[pl.ds(row0, tq), :] = slope_ref[h] * jnp.abs(tpos - spos)

    lv = lam_ref[...]
    lam = (jnp.exp(jnp.sum(lv[0:1] * lv[1:2], axis=-1, keepdims=True))
           - jnp.exp(jnp.sum(lv[2:3] * lv[3:4], axis=-1, keepdims=True)) + lambda_init)

    _cast_blocks(w_refs, wo_refs)
    for u in range(ATTN_B_SUBS):
        sub_rows = slice(u * ts, (u + 1) * ts)
        bias_rows = pl.ds(row0 + u * ts, ts)
        pv = [None, None]
        l_part = [None, None]
        m_run = [None, None]
        for n in range(seq // ATTN_B_KCHUNK):
            k0 = n * ATTN_B_KCHUNK
            keys = slice(k0, k0 + ATTN_B_KCHUNK)
            for c in range(2):
                m_rep = m_ref[u, c]
                ps = [jnp.exp2(s_ref[u, c, :, k0 + j * LANES:k0 + (j + 1) * LANES] - m_rep)
                      for j in range(ATTN_B_KCHUNK // LANES)]
                l_new = functools.reduce(jnp.add, ps)
                l_part[c] = l_new if n == 0 else l_part[c] + l_new
                pv_new = _dot(jnp.concatenate(ps, axis=1).astype(BF16), v_ref[keys, :])
                pv[c] = pv_new if n == 0 else pv[c] + pv_new
            for c in range(2):
                dims = slice(c * B_DH, (c + 1) * B_DH)
                s = _dot_nt(q_ref[sub_rows, dims], k_ref[keys, dims]) - bias_ref[bias_rows, keys]
                s_ref[u, c, :, keys] = s
                part = functools.reduce(
                    jnp.maximum, [s[:, j * LANES:(j + 1) * LANES] for j in range(ATTN_B_KCHUNK // LANES)])
                m_run[c] = part if n == 0 else jnp.maximum(m_run[c], part)
        l1 = jnp.sum(l_part[0], axis=-1, keepdims=True)
        l2 = jnp.sum(l_part[1], axis=-1, keepdims=True)
        o = pv[0] * (1.0 / l1) - pv[1] * (lam / l2)
        o_ref[sub_rows, :] = (_row_rms(o, sub_ref[...]) * (1.0 - lambda_init)).astype(o_ref.dtype)
        for c in range(2):
            m_ref[u, c] = jnp.broadcast_to(jnp.max(m_run[c], axis=-1, keepdims=True), (ts, LANES))


def _attn_b(qkv, slopes, lam_vecs, subln, batch, seq, lambda_init, cast_weights):
    tq = ATTN_B_TQ
    ts = tq // ATTN_B_SUBS
    nq = seq // tq
    e = 2 * B_DH
    n_steps = B_H * batch * nq + 1

    def q_map(t):
        h, b, i = _attn_b_block(t, batch, nq)
        return b * nq + i, h

    def k_map(t):
        h, b, _ = _attn_b_block(t, batch, nq)
        return b, B_H + h

    def v_map(t):
        h, b, _ = _attn_b_block(t - 1, batch, nq)
        return b, 2 * B_H + h

    plan = _cast_plan(cast_weights, n_steps)
    cast_in, cast_out, cast_shapes = _cast_specs(plan, lambda t: t)
    outs = pl.pallas_call(
        functools.partial(_attn_b_kernel, lambda_init=lambda_init, batch=batch, nq=nq, n_cast=len(plan)),
        grid=(n_steps,),
        in_specs=[
            pl.BlockSpec(memory_space=pltpu.SMEM),
            pl.BlockSpec((4, B_DH), lambda t: (0, 0)),
            pl.BlockSpec((1, e), lambda t: (0, 0)),
            pl.BlockSpec((tq, e), q_map),
            pl.BlockSpec((seq, e), k_map),
            pl.BlockSpec((seq, e), v_map),
        ] + cast_in,
        out_specs=[pl.BlockSpec((tq, e), lambda t: q_map(t - 1))] + cast_out,
        out_shape=[jax.ShapeDtypeStruct((batch * seq, B_W), BF16)] + cast_shapes,
        scratch_shapes=[pltpu.VMEM((seq, seq), F32),
                        pltpu.VMEM((ATTN_B_SUBS, 2, ts, seq), F32),
                        pltpu.VMEM((ATTN_B_SUBS, 2, ts, LANES), F32)],
        compiler_params=_params("arbitrary"),
        name="attn_b",
    )(slopes, lam_vecs, subln.reshape(1, e), qkv, qkv, qkv, *[p[0] for p in plan])
    return outs[0], outs[1:]


def _proj_res_kernel(a_ref, w_ref, r_ref, o_ref):
    for c in range(o_ref.shape[1] // PROJ_SUB):
        cols = slice(c * PROJ_SUB, (c + 1) * PROJ_SUB)
        o_ref[:, cols] = r_ref[:, cols] + _dot(a_ref[...], w_ref[:, cols])


def _proj_res(a, w, res):
    m, kdim = a.shape
    n = w.shape[1]
    tm = PROJ_TM
    return pl.pallas_call(
        _proj_res_kernel,
        grid=(m // tm,),
        in_specs=[
            pl.BlockSpec((tm, kdim), lambda i: (i, 0)),
            pl.BlockSpec((kdim, n), lambda i: (0, 0)),
            pl.BlockSpec((tm, n), lambda i: (i, 0)),
        ],
        out_specs=pl.BlockSpec((tm, n), lambda i: (i, 0)),
        out_shape=jax.ShapeDtypeStruct((m, n), F32),
        compiler_params=_params("parallel"),
        name="proj_res",
    )(a, w, res)


def _ffn_kernel(x_ref, g_ref, wgu_ref, wo_ref, o_ref, hn_ref):
    f = pl.program_id(1)
    tf = wo_ref.shape[0]

    def chunk(base_ref):
        gu = _dot(hn_ref[...], wgu_ref[...])
        gate = gu[:, :tf]
        act = (gate * jax.nn.sigmoid(gate) * gu[:, tf:]).astype(BF16)
        o_ref[...] = base_ref[...] + _dot(act, wo_ref[...])

    @pl.when(f == 0)
    def _():
        hn_ref[...] = _row_rms(x_ref[...], g_ref[...]).astype(BF16)
        chunk(x_ref)

    @pl.when(f > 0)
    def _():
        chunk(o_ref)


def _ffn(x, gain, w_in_chunks, w_out):
    m, d = x.shape
    tm, tf = FFN_TM, FFN_TF
    nf = D_FF // tf
    return pl.pallas_call(
        _ffn_kernel,
        grid=(m // tm, nf),
        in_specs=[
            pl.BlockSpec((tm, d), lambda i, f: (i, 0)),
            pl.BlockSpec((1, d), lambda i, f: (0, 0)),
            pl.BlockSpec((None, d, 2 * tf), lambda i, f: (f, 0, 0)),
            pl.BlockSpec((tf, d), lambda i, f: (f, 0)),
        ],
        out_specs=pl.BlockSpec((tm, d), lambda i, f: (i, 0)),
        out_shape=jax.ShapeDtypeStruct((m, d), F32),
        scratch_shapes=[pltpu.VMEM((tm, d), BF16)],
        compiler_params=_params("parallel", "arbitrary"),
        name="ffn",
    )(x, gain.reshape(1, d), w_in_chunks, w_out)


def _ple_kernel(h_ref, p_ref, gn_ref, wg_ref, wp_ref, pn_ref, o_ref):
    for u in range(PLE_ROW_SPLIT):
        rows = slice(u * (o_ref.shape[0] // PLE_ROW_SPLIT), (u + 1) * (o_ref.shape[0] // PLE_ROW_SPLIT))
        hn = _row_rms(h_ref[rows, :], gn_ref[...]).astype(BF16)
        pe = _row_rms(_dot(p_ref[rows, :].astype(BF16), wp_ref[...]), pn_ref[...])
        for c in range(o_ref.shape[1] // PLE_SUB):
            cols = slice(c * PLE_SUB, (c + 1) * PLE_SUB)
            gate = jax.nn.sigmoid(_dot(hn, wg_ref[:, cols]))
            o_ref[rows, cols] = h_ref[rows, cols] + pe[:, cols] * gate


def _ple(h, p_all, layer, gate_norm, w_gate, w_proj, post_norm):
    m, d = h.shape
    tm = PLE_TM
    return pl.pallas_call(
        _ple_kernel,
        grid=(m // tm,),
        in_specs=[
            pl.BlockSpec((tm, d), lambda i: (i, 0)),
            pl.BlockSpec((None, tm, PLE_DIM), lambda i: (layer, i, 0)),
            pl.BlockSpec((1, d), lambda i: (0, 0)),
            pl.BlockSpec((d, d), lambda i: (0, 0)),
            pl.BlockSpec((PLE_DIM, d), lambda i: (0, 0)),
            pl.BlockSpec((1, d), lambda i: (0, 0)),
        ],
        out_specs=pl.BlockSpec((tm, d), lambda i: (i, 0)),
        out_shape=jax.ShapeDtypeStruct((m, d), F32),
        compiler_params=_params("parallel"),
        name="ple",
    )(h, p_all, gate_norm.reshape(1, d), w_gate, w_proj, post_norm.reshape(1, d))


def _alibi_slopes(n):
    return jnp.exp2(-8.0 * jnp.arange(1, n + 1, dtype=F32) / n)


def kernel(x, p, attn_norm, ffn_norm, a_w_qkv, a_q_norm, a_k_norm, a_sink, a_w_o, b_w_qkv, b_q_norm, b_k_norm, b_lambda, b_subln, b_w_o, w_ffn_in, w_ffn_out, ple_w_proj, ple_post_norm, ple_gate_norm, ple_w_gate):
    batch, seq, d = x.shape
    m = batch * seq
    h = x.reshape(m, d)
    p_all = p.reshape(DEPTH, m, PLE_DIM)
    w_qkv = a_w_qkv[0].astype(BF16)
    for i in range(DEPTH):
        j = i // 2
        if i % 2 == 0:
            col_gain = jnp.concatenate([jnp.tile(a_q_norm[j] * (A_DH ** -0.5 * LOG2E), A_HQ),
                                        jnp.tile(a_k_norm[j], A_KV), jnp.ones((A_NK,), F32)])
            col_mask = jnp.concatenate([jnp.ones((A_NQ + A_NK,), F32), jnp.zeros((A_NK,), F32)])
            qkv = _qkv_proj(h, attn_norm[i], w_qkv, col_gain, col_mask, A_DH,
                            QKV_A_TM, QKV_A_TN, QKV_A_SUB, A_QKV // QKV_A_TN)
            mix, (w_o, w_in, w_out, w_gate, w_proj, w_qkv) = _attn_a(
                qkv, _alibi_slopes(A_HQ) * LOG2E, a_sink[j].astype(F32) * LOG2E, batch, seq,
                [(a_w_o, j, False), (w_ffn_in, i, True), (w_ffn_out, i, False), (ple_w_gate, i, False),
                 (ple_w_proj, i, False), (b_w_qkv, j, False)])
        else:
            lambda_init = 0.8 - 0.6 * math.exp(-0.3 * i)
            col_gain = jnp.concatenate([jnp.tile(b_q_norm[j] * (B_DH ** -0.5 * LOG2E), 2 * B_H),
                                        jnp.tile(b_k_norm[j], 2 * B_H), jnp.ones((B_W,), F32)])
            col_mask = jnp.concatenate([jnp.ones((2 * B_W,), F32), jnp.zeros((B_W,), F32)])
            qkv = _qkv_proj(h, attn_norm[i], w_qkv, col_gain, col_mask, B_DH,
                            QKV_B_TM, QKV_B_TN, QKV_B_SUB, 2 * B_W // QKV_B_TN)
            mix, (w_o, w_in, w_out, w_gate, w_proj) = _attn_b(
                qkv, _alibi_slopes(B_H) * LOG2E, b_lambda[j], b_subln[j], batch, seq, lambda_init,
                [(b_w_o, j, False), (w_ffn_in, i, True), (w_ffn_out, i, False), (ple_w_gate, i, False),
                 (ple_w_proj, i, False)])
        h = _proj_res(mix, w_o, h)
        h = _ffn(h, ffn_norm[i], w_in, w_out)
        h = _ple(h, p_all, i, ple_gate_norm[i], w_gate, w_proj, ple_post_norm[i])
    return h.reshape(batch, seq, d)
```
